```python
import jax, jax.numpy as jnp
from jax import lax
import numpy as np

D_MODEL = 1024
BATCH = 2
SEQ = 8192
DEPTH = 1

D_MIX = D_MODEL
D_POOL = D_MIX // 2
D_LRU = D_MIX - D_POOL
POOL_WINDOWS = (2, 4, 8, 16)
N_POOL_GROUPS = len(POOL_WINDOWS)
POOL_GROUP_WIDTH = D_POOL // N_POOL_GROUPS
LRU_HEADS = 8
LRU_BLOCK = D_LRU // LRU_HEADS
CONV_WIDTH = 4
LRU_C = 8.0
D_FF = 4 * D_MODEL
PLE_DIM = 256
RMS_EPS = 1e-6
D_IN_PROJ = D_POOL + D_LRU + D_LRU

kernel_name = "hybrid_pool_rglru_block"


def rms_norm(x, g):
    xf = x.astype(jnp.float32)
    y = xf * lax.rsqrt(jnp.mean(xf * xf, axis=-1, keepdims=True) + RMS_EPS)
    return (y * g.astype(jnp.float32)).astype(x.dtype)


def multiscale_pool_mixer(u, pool_w, pool_b, pool_scale):
    B, S, _ = u.shape
    ug = u.reshape(B, S, N_POOL_GROUPS, POOL_GROUP_WIDTH).astype(jnp.float32)
    csum = jnp.cumsum(ug, axis=1)
    t = jnp.arange(S)
    outs = []
    for g, w in enumerate(POOL_WINDOWS):
        c = csum[:, :, g]
        c_lag = jnp.pad(c, ((0, 0), (w, 0), (0, 0)))[:, :S]
        count = jnp.minimum(t + 1, w).astype(jnp.float32)
        outs.append((c - c_lag) / count[None, :, None] - ug[:, :, g])
    d = jnp.stack(outs, axis=2).astype(u.dtype)
    y = jnp.einsum('bsgc,gcd->bsgd', d, pool_w) + pool_b
    return y.reshape(B, S, D_POOL) * pool_scale


def causal_depthwise_conv(u, w, b):
    S = u.shape[1]
    upad = jnp.pad(u, ((0, 0), (CONV_WIDTH - 1, 0), (0, 0)))
    y = b
    for k in range(CONV_WIDTH):
        y = y + upad[:, k:k + S] * w[k]
    return y


def _linear_recurrence_combine(c1, c2):
    a1, b1 = c1
    a2, b2 = c2
    return a1 * a2, a2 * b1 + b2


def rg_lru(u, gate_a_w, gate_a_b, gate_x_w, gate_x_b, lru_L):
    B, S, W = u.shape
    uh = u.reshape(B, S, LRU_HEADS, LRU_BLOCK)
    r = jax.nn.sigmoid(jnp.einsum('bshi,hij->bshj', uh, gate_a_w) + gate_a_b).reshape(B, S, W)
    i = jax.nn.sigmoid(jnp.einsum('bshi,hij->bshj', uh, gate_x_w) + gate_x_b).reshape(B, S, W)
    log_a = LRU_C * r.astype(jnp.float32) * jax.nn.log_sigmoid(lru_L.astype(jnp.float32))
    a = jnp.exp(log_a)
    mult = jnp.sqrt(-jnp.expm1(2.0 * log_a))
    is_first = (jnp.arange(S) == 0)[None, :, None]
    mult = jnp.where(is_first, 1.0, mult)
    b = mult * (i * u).astype(jnp.float32)
    _, h = lax.associative_scan(_linear_recurrence_combine, (a, b), axis=1)
    return h.astype(u.dtype)


def setup_inputs(seed: int = 0) -> dict:
    key = jax.random.key(seed)
    ks = jax.random.split(key, 24)
    f32 = jnp.float32
    nrm = lambda k, shape, scale: jax.random.normal(k, shape, f32) * scale
    gain = lambda k, shape: 1.0 + 0.05 * jax.random.normal(k, shape, f32)
    L = DEPTH
    rad = jnp.sqrt(jax.random.uniform(ks[12], (L, D_LRU), f32, 0.9 ** 2, 0.999 ** 2))
    lru_L = jnp.log(rad) - jnp.log1p(-rad)
    return {
        "x": jax.random.normal(ks[0], (BATCH, SEQ, D_MODEL), f32),
        "p": jax.random.normal(ks[1], (DEPTH, BATCH, SEQ, PLE_DIM), f32),
        "norm_mix_g": gain(ks[2], (L, D_MODEL)),
        "w_in": nrm(ks[3], (L, D_MODEL, D_IN_PROJ), D_MODEL ** -0.5),
        "pool_w": nrm(ks[4], (L, N_POOL_GROUPS, POOL_GROUP_WIDTH, POOL_GROUP_WIDTH), POOL_GROUP_WIDTH ** -0.5),
        "pool_b": nrm(ks[5], (L, N_POOL_GROUPS, POOL_GROUP_WIDTH), 0.01),
        "pool_scale": gain(ks[6], (L, D_POOL)),
        "conv_w": nrm(ks[7], (L, CONV_WIDTH, D_LRU), CONV_WIDTH ** -0.5),
        "conv_b": nrm(ks[8], (L, D_LRU), 0.01),
        "gate_a_w": nrm(ks[9], (L, LRU_HEADS, LRU_BLOCK, LRU_BLOCK), LRU_BLOCK ** -0.5),
        "gate_a_b": nrm(ks[10], (L, LRU_HEADS, LRU_BLOCK), 0.01),
        "gate_x_w": nrm(ks[11], (L, LRU_HEADS, LRU_BLOCK, LRU_BLOCK), LRU_BLOCK ** -0.5),
        "gate_x_b": nrm(ks[13], (L, LRU_HEADS, LRU_BLOCK), 0.01),
        "lru_L": lru_L,
        "w_out": nrm(ks[14], (L, D_MIX, D_MODEL), D_MIX ** -0.5),
        "norm_mlp_g": gain(ks[15], (L, D_MODEL)),
        "w_up": nrm(ks[16], (L, D_MODEL, D_FF), D_MODEL ** -0.5),
        "w_down": nrm(ks[17], (L, D_FF, D_MODEL), D_FF ** -0.5),
        "norm_ple_g": gain(ks[18], (L, D_MODEL)),
        "w_ple_gate": nrm(ks[19], (L, D_MODEL, D_MODEL), D_MODEL ** -0.5),
        "b_ple_gate": nrm(ks[20], (L, D_MODEL), 0.01),
        "w_ple_proj": nrm(ks[21], (L, PLE_DIM, D_MODEL), PLE_DIM ** -0.5),
        "norm_final_g": gain(ks[22], (D_MODEL,)),
    }


def reference(x, p, norm_mix_g, w_in, pool_w, pool_b, pool_scale, conv_w, conv_b,
              gate_a_w, gate_a_b, gate_x_w, gate_x_b, lru_L, w_out, norm_mlp_g,
              w_up, w_down, norm_ple_g, w_ple_gate, b_ple_gate, w_ple_proj, norm_final_g):
    h = x
    for l in range(DEPTH):
        z = rms_norm(h, norm_mix_g[l])
        proj = z @ w_in[l]
        u_pool = proj[..., :D_POOL]
        u_lru = proj[..., D_POOL:D_POOL + D_LRU]
        u_gate = proj[..., D_POOL + D_LRU:]
        y_pool = multiscale_pool_mixer(u_pool, pool_w[l], pool_b[l], pool_scale[l])
        xb = causal_depthwise_conv(u_lru, conv_w[l], conv_b[l])
        y_lru = rg_lru(xb, gate_a_w[l], gate_a_b[l], gate_x_w[l], gate_x_b[l], lru_L[l])
        y_lru = y_lru * jax.nn.gelu(u_gate)
        h = h + jnp.concatenate([y_pool, y_lru], axis=-1) @ w_out[l]
        z = rms_norm(h, norm_mlp_g[l])
        h = h + jnp.square(jax.nn.relu(z @ w_up[l])) @ w_down[l]
        z = rms_norm(h, norm_ple_g[l])
        gate = jax.nn.sigmoid(z @ w_ple_gate[l] + b_ple_gate[l])
        h = h + gate * (p[l] @ w_ple_proj[l])
    return rms_norm(h, norm_final_g)
```

```python
import functools

import jax
import jax.numpy as jnp
from jax import lax
from jax.experimental import pallas as pl
from jax.experimental.pallas import tpu as pltpu

D_MODEL = 1024
D_POOL = 512
D_LRU = 512
POOL_WINDOWS = (2, 4, 8, 16)
POOL_GROUP_WIDTH = D_POOL // len(POOL_WINDOWS)
LRU_HEADS = 8
LRU_BLOCK = D_LRU // LRU_HEADS
CONV_WIDTH = 4
LRU_C = 8.0
D_FF = 4 * D_MODEL
PLE_DIM = 256
RMS_EPS = 1e-6

SUBLANES = 8
MXU_DIM = 256
VMEM_LIMIT_BYTES = 56 * 1024 * 1024

SEQ_TILE = 512
FF_CHUNK = 1024
POOL_HALO = max(POOL_WINDOWS)
CONV_HALO = SUBLANES
SCAN_UNROLL = 4

F32 = jnp.float32
BF16 = jnp.bfloat16


def _rms_norm(x, g):
    ms = jnp.mean(x * x, axis=-1, keepdims=True)
    return x * lax.rsqrt(ms + RMS_EPS) * g


def _dot(a, b):
    return jnp.dot(a, b, preferred_element_type=F32)


def _log_sigmoid(x):
    return -(jnp.maximum(-x, 0.0) + jnp.log1p(jnp.exp(-jnp.abs(x))))


def _block_kernel(
    x_ref, p_ref, g_mix_ref, w_in_ref, w_pool_ref, pool_b_ref, pool_scale_ref,
    conv_w_ref, conv_b_ref, w_gate_ref, gate_a_b_ref, gate_x_b_ref, lru_l_ref,
    w_out_ref, g_mlp_ref, w_up_ref, w_down_ref, g_ple_ref, w_ple_gate_ref,
    b_ple_gate_ref, w_ple_proj_ref, g_final_ref,
    o_ref,
    pool_halo, conv_halo, h_carry, a_sc, b_sc,
):
    tile = pl.program_id(1)
    t_rows = x_ref.shape[0]

    @pl.when(tile == 0)
    def _():
        pool_halo[...] = jnp.zeros_like(pool_halo)
        conv_halo[...] = jnp.zeros_like(conv_halo)
        h_carry[...] = jnp.zeros_like(h_carry)

    x = x_ref[...]
    z = _rms_norm(x, g_mix_ref[...]).astype(BF16)
    u_pool = _dot(z, w_in_ref[:, 0:D_POOL])
    u_lru = _dot(z, w_in_ref[:, D_POOL:D_POOL + D_LRU])
    u_gate = _dot(z, w_in_ref[:, D_POOL + D_LRU:])

    t_idx = tile * t_rows + lax.broadcasted_iota(jnp.int32, (t_rows, 1), 0)

    ext = jnp.concatenate([pool_halo[...], u_pool], axis=0)
    pool_halo[...] = u_pool[t_rows - POOL_HALO:, :]
    d_groups = []
    for g, w in enumerate(POOL_WINDOWS):
        ug = ext[:, g * POOL_GROUP_WIDTH:(g + 1) * POOL_GROUP_WIDTH]
        s = ug
        k = 1
        while k < w:
            s = s + pltpu.roll(s, k, axis=0)
            k *= 2
        count = jnp.minimum(t_idx + 1, w).astype(F32)
        d_groups.append(s[POOL_HALO:] / count - ug[POOL_HALO:])
    d = jnp.concatenate(d_groups, axis=1).astype(BF16)
    y_pool = jnp.concatenate(
        [_dot(d[:, h * MXU_DIM:(h + 1) * MXU_DIM], w_pool_ref[h])
         for h in range(D_POOL // MXU_DIM)], axis=1)
    y_pool = (y_pool + pool_b_ref[...]) * pool_scale_ref[...]

    ext = jnp.concatenate([conv_halo[...], u_lru], axis=0)
    conv_halo[...] = u_lru[t_rows - CONV_HALO:, :]
    conv_w = conv_w_ref[...]
    xb = conv_b_ref[...]
    for k in range(CONV_WIDTH):
        lag = CONV_WIDTH - 1 - k
        tap = ext if lag == 0 else pltpu.roll(ext, lag, axis=0)
        xb = xb + tap * conv_w[k:k + 1, :]
    xb = xb[CONV_HALO:]

    xb_bf = xb.astype(BF16)
    r_parts, i_parts = [], []
    for h in range(D_LRU // MXU_DIM):
        gh = _dot(xb_bf[:, h * MXU_DIM:(h + 1) * MXU_DIM], w_gate_ref[h])
        r_parts.append(gh[:, :MXU_DIM])
        i_parts.append(gh[:, MXU_DIM:])
    r = jax.nn.sigmoid(jnp.concatenate(r_parts, axis=1) + gate_a_b_ref[...])
    i = jax.nn.sigmoid(jnp.concatenate(i_parts, axis=1) + gate_x_b_ref[...])
    log_a = LRU_C * r * _log_sigmoid(lru_l_ref[...])
    a = jnp.exp(log_a)
    mult = jnp.sqrt(1.0 - a * a)
    mult = jnp.where(t_idx == 0, 1.0, mult)
    a_sc[...] = a
    b_sc[...] = mult * (i * xb)

    sub = lax.broadcasted_iota(jnp.int32, (SUBLANES, D_LRU), 0)

    def scan_group(k, h_prev):
        off = pl.multiple_of(k * SUBLANES, SUBLANES)
        av = a_sc[pl.ds(off, SUBLANES), :]
        bv = b_sc[pl.ds(off, SUBLANES), :]
        shift = 1
        while shift < SUBLANES:
            keep = sub >= shift
            bv = jnp.where(keep, av * pltpu.roll(bv, shift, axis=0) + bv, bv)
            av = jnp.where(keep, av * pltpu.roll(av, shift, axis=0), av)
            shift *= 2
        h = av * h_prev + bv
        b_sc[pl.ds(off, SUBLANES), :] = h
        return h[SUBLANES - 1:SUBLANES, :]

    h_last = lax.fori_loop(0, t_rows // SUBLANES, scan_group, h_carry[0:1, :],
                           unroll=SCAN_UNROLL)
    h_carry[0:1, :] = h_last
    y_lru = b_sc[...] * jax.nn.gelu(u_gate)

    h1 = x + (_dot(y_pool.astype(BF16), w_out_ref[0:D_POOL, :])
              + _dot(y_lru.astype(BF16), w_out_ref[D_POOL:, :]))

    z = _rms_norm(h1, g_mlp_ref[...]).astype(BF16)
    acc = None
    for c in range(D_FF // FF_CHUNK):
        up = _dot(z, w_up_ref[:, c * FF_CHUNK:(c + 1) * FF_CHUNK])
        act = jnp.square(jnp.maximum(up, 0.0)).astype(BF16)
        down = _dot(act, w_down_ref[c * FF_CHUNK:(c + 1) * FF_CHUNK, :])
        acc = down if acc is None else acc + down
    h2 = h1 + acc

    z = _rms_norm(h2, g_ple_ref[...]).astype(BF16)
    gate = jax.nn.sigmoid(_dot(z, w_ple_gate_ref[...]) + b_ple_gate_ref[...])
    h3 = h2 + gate * _dot(p_ref[...].astype(BF16), w_ple_proj_ref[...])

    o_ref[...] = _rms_norm(h3, g_final_ref[...])


def _block_diag(w, group):
    heads, n, _ = w.shape
    w = w.reshape(heads // group, group, n, n)
    eye = jnp.eye(group, dtype=w.dtype)
    out = jnp.einsum('bgij,gh->bgihj', w, eye)
    return out.reshape(heads // group, group * n, group * n)


def _resident(shape):
    zeros = (0,) * len(shape)
    return pl.BlockSpec(shape, lambda b, t: zeros, pipeline_mode=pl.Buffered(1))


@jax.jit
def kernel(x, p, norm_mix_g, w_in, pool_w, pool_b, pool_scale, conv_w, conv_b, gate_a_w, gate_a_b, gate_x_w, gate_x_b, lru_L, w_out, norm_mlp_g, w_up, w_down, norm_ple_g, w_ple_gate, b_ple_gate, w_ple_proj, norm_final_g):
    batch, seq, d_model = x.shape
    assert d_model == D_MODEL and seq % SEQ_TILE == 0
    assert SEQ_TILE >= POOL_HALO and SEQ_TILE % SUBLANES == 0
    assert p.shape[0] == 1 and w_in.shape[0] == 1, "single-layer block"

    row = lambda v: v.reshape(1, -1).astype(F32)
    w_pool = _block_diag(pool_w[0], MXU_DIM // POOL_GROUP_WIDTH).astype(BF16)
    w_gate = jnp.concatenate(
        [_block_diag(gate_a_w[0], MXU_DIM // LRU_BLOCK),
         _block_diag(gate_x_w[0], MXU_DIM // LRU_BLOCK)], axis=-1).astype(BF16)

    operands = (
        x, p[0], row(norm_mix_g[0]), w_in[0].astype(BF16), w_pool, row(pool_b[0]),
        row(pool_scale[0]), conv_w[0].astype(F32), row(conv_b[0]), w_gate,
        row(gate_a_b[0]), row(gate_x_b[0]), row(lru_L[0]), w_out[0].astype(BF16),
        row(norm_mlp_g[0]), w_up[0].astype(BF16), w_down[0].astype(BF16),
        row(norm_ple_g[0]), w_ple_gate[0].astype(BF16), row(b_ple_gate[0]),
        w_ple_proj[0].astype(BF16), row(norm_final_g),
    )
    tile_spec = lambda width: pl.BlockSpec((None, SEQ_TILE, width), lambda b, t: (b, t, 0))
    in_specs = [tile_spec(D_MODEL), tile_spec(PLE_DIM)] + [_resident(a.shape) for a in operands[2:]]

    return pl.pallas_call(
        _block_kernel,
        grid=(batch, seq // SEQ_TILE),
        in_specs=in_specs,
        out_specs=tile_spec(D_MODEL),
        out_shape=jax.ShapeDtypeStruct(x.shape, x.dtype),
        scratch_shapes=[
            pltpu.VMEM((POOL_HALO, D_POOL), F32),
            pltpu.VMEM((CONV_HALO, D_LRU), F32),
            pltpu.VMEM((SUBLANES, D_LRU), F32),
            pltpu.VMEM((SEQ_TILE, D_LRU), F32),
            pltpu.VMEM((SEQ_TILE, D_LRU), F32),
        ],
        compiler_params=pltpu.CompilerParams(
            dimension_semantics=("arbitrary", "arbitrary"),
            vmem_limit_bytes=VMEM_LIMIT_BYTES,
        ),
        name="hybrid_block",
    )(*operands)
```

```python
import functools

import jax
import jax.numpy as jnp
from jax import lax
from jax.experimental import pallas as pl
from jax.experimental.pallas import tpu as pltpu

D_MODEL = 1024
D_POOL = 512
D_LRU = 512
POOL_WINDOWS = (2, 4, 8, 16)
POOL_GROUP_WIDTH = D_POOL // len(POOL_WINDOWS)
LRU_HEADS = 8
LRU_BLOCK = D_LRU // LRU_HEADS
CONV_WIDTH = 4
LRU_C = 8.0
D_FF = 4 * D_MODEL
PLE_DIM = 256
RMS_EPS = 1e-6

SUBLANES = 8
MXU_DIM = 256
VMEM_LIMIT_BYTES = 56 * 1024 * 1024

SEQ_TILE = 512
DOT_N = 2 * MXU_DIM
ROW_BLOCKS = 4
POOL_HALO = max(POOL_WINDOWS)
CONV_HALO = SUBLANES

_FF_BLOCKS = D_FF // DOT_N
STAGE_ORDER = ("T" + "C" * (_FF_BLOCKS // 2) + "T" * (2 * ROW_BLOCKS + 2)
               + "C" * (_FF_BLOCKS - _FF_BLOCKS // 2) + "T" * (2 * ROW_BLOCKS)
               + "C" * _FF_BLOCKS + "T" + "C")

F32 = jnp.float32
BF16 = jnp.bfloat16


def _rms_norm(x, g):
    ms = jnp.mean(x * x, axis=-1, keepdims=True)
    return x * lax.rsqrt(ms + RMS_EPS) * g


def _dot(a, w_ref, blocks=None, rows=slice(None)):
    blocks = range(w_ref.shape[0]) if blocks is None else blocks
    outs = [jnp.dot(a, w_ref[j, rows, :], preferred_element_type=F32) for j in blocks]
    return outs[0] if len(outs) == 1 else jnp.concatenate(outs, axis=1)


def _head_dot(a, w_ref):
    return [jnp.dot(a[:, h * MXU_DIM:(h + 1) * MXU_DIM], w_ref[h], preferred_element_type=F32)
            for h in range(w_ref.shape[0])]


def _log_sigmoid(x):
    return -(jnp.maximum(-x, 0.0) + jnp.log1p(jnp.exp(-jnp.abs(x))))


def _linear_recurrence(a, b, h_prev):
    t_rows, width = a.shape
    sub = lax.broadcasted_iota(jnp.int32, (SUBLANES, width), 0)
    groups = []
    for k in range(t_rows // SUBLANES):
        av = a[k * SUBLANES:(k + 1) * SUBLANES]
        bv = b[k * SUBLANES:(k + 1) * SUBLANES]
        shift = 1
        while shift < SUBLANES:
            keep = sub >= shift
            bv = jnp.where(keep, av * pltpu.roll(bv, shift, axis=0) + bv, bv)
            av = jnp.where(keep, av * pltpu.roll(av, shift, axis=0), av)
            shift *= 2
        h = av * h_prev + bv
        groups.append(h)
        h_prev = h[SUBLANES - 1:SUBLANES, :]
    return jnp.concatenate(groups, axis=0), h_prev


def _token_mixer(x, first_tile, row0, w, pool_halo, conv_halo, h_carry):
    t_rows = x.shape[0]
    row_blocks = [slice(k * t_rows // ROW_BLOCKS, (k + 1) * t_rows // ROW_BLOCKS)
                  for k in range(ROW_BLOCKS)]

    @pl.when(first_tile)
    def _():
        pool_halo[...] = jnp.zeros_like(pool_halo)
        conv_halo[...] = jnp.zeros_like(conv_halo)
        h_carry[...] = jnp.zeros_like(h_carry)

    z = _rms_norm(x, w["g_mix"][...]).astype(BF16)
    u_pool = _dot(z, w["w_in"], blocks=[0])
    u_lru = _dot(z, w["w_in"], blocks=[1])
    u_gate = _dot(z, w["w_in"], blocks=[2])
    yield

    t_idx = row0 + lax.broadcasted_iota(jnp.int32, (t_rows, 1), 0)

    ext = jnp.concatenate([pool_halo[...], u_pool], axis=0)
    pool_halo[...] = u_pool[t_rows - POOL_HALO:, :]
    d_blocks = []
    for rows in row_blocks:
        ext_rows = ext[rows.start:rows.stop + POOL_HALO]
        d_groups = []
        for g, win in enumerate(POOL_WINDOWS):
            ug = ext_rows[:, g * POOL_GROUP_WIDTH:(g + 1) * POOL_GROUP_WIDTH]
            s = ug
            k = 1
            while k < win:
                s = s + pltpu.roll(s, k, axis=0)
                k *= 2
            count = jnp.minimum(t_idx[rows] + 1, win).astype(F32)
            d_groups.append(s[POOL_HALO:] / count - ug[POOL_HALO:])
        d_blocks.append(jnp.concatenate(d_groups, axis=1).astype(BF16))
        yield
    d = jnp.concatenate(d_blocks, axis=0)
    y_pool = jnp.concatenate(_head_dot(d, w["w_pool"]), axis=1)
    y_pool = ((y_pool + w["pool_b"][...]) * w["pool_scale"][...]).astype(BF16)
    yield

    ext = jnp.concatenate([conv_halo[...], u_lru], axis=0)
    conv_halo[...] = u_lru[t_rows - CONV_HALO:, :]
    conv_w = w["conv_w"][...]
    xb_blocks = []
    for rows in row_blocks:
        ext_rows = ext[rows.start:rows.stop + CONV_HALO]
        xb = w["conv_b"][...]
        for k in range(CONV_WIDTH):
            lag = CONV_WIDTH - 1 - k
            tap = ext_rows if lag == 0 else pltpu.roll(ext_rows, lag, axis=0)
            xb = xb + tap * conv_w[k:k + 1, :]
        xb_blocks.append(xb[CONV_HALO:])
        yield
    xb = jnp.concatenate(xb_blocks, axis=0)

    gate_parts = _head_dot(xb.astype(BF16), w["w_gate"])
    r_pre = jnp.concatenate([g[:, :MXU_DIM] for g in gate_parts], axis=1)
    i_pre = jnp.concatenate([g[:, MXU_DIM:] for g in gate_parts], axis=1)
    yield
    log_sig_l = _log_sigmoid(w["lru_l"][...])
    a_blocks, b_blocks = [], []
    for rows in row_blocks:
        r = jax.nn.sigmoid(r_pre[rows] + w["gate_a_b"][...])
        i = jax.nn.sigmoid(i_pre[rows] + w["gate_x_b"][...])
        a = jnp.exp(LRU_C * r * log_sig_l)
        mult = jnp.where(t_idx[rows] == 0, 1.0, jnp.sqrt(1.0 - a * a))
        a_blocks.append(a)
        b_blocks.append(mult * (i * xb[rows]))
        yield

    y_blocks, h_last = [], h_carry[0:1, :]
    for k, rows in enumerate(row_blocks):
        h, h_last = _linear_recurrence(a_blocks[k], b_blocks[k], h_last)
        y_blocks.append((h * jax.nn.gelu(u_gate[rows])).astype(BF16))
        yield
    h_carry[0:1, :] = h_last
    y_lru = jnp.concatenate(y_blocks, axis=0)

    return x + (_dot(y_pool, w["w_out"], rows=slice(0, D_POOL))
                + _dot(y_lru, w["w_out"], rows=slice(D_POOL, D_POOL + D_LRU)))


def _channel_mixers(h1_ref, p, w):
    z = _rms_norm(h1_ref[...], w["g_mlp"][...]).astype(BF16)
    acts = []
    for c in range(D_FF // DOT_N):
        up = _dot(z, w["w_up"], blocks=[c])
        acts.append(jnp.square(jnp.maximum(up, 0.0)).astype(BF16))
        yield
    acc = None
    for c, act in enumerate(acts):
        down = _dot(act, w["w_down"], rows=slice(c * DOT_N, (c + 1) * DOT_N))
        acc = down if acc is None else acc + down
        yield
    h2 = h1_ref[...] + acc

    z = _rms_norm(h2, w["g_ple"][...]).astype(BF16)
    gate = jax.nn.sigmoid(_dot(z, w["w_ple_gate"]) + w["b_ple_gate"][...])
    h3 = h2 + gate * _dot(p.astype(BF16), w["w_ple_proj"])
    return _rms_norm(h3, w["g_final"][...])


def _trace_stages(token_stages, channel_stages, order):
    generators = {"T": token_stages, "C": channel_stages}
    results = {}

    def advance(name):
        if name not in results:
            try:
                next(generators[name])
            except StopIteration as stop:
                results[name] = stop.value

    for name in order:
        advance(name)
    for name in generators:
        while name not in results:
            advance(name)
    return results["T"], results["C"]


_WEIGHT_NAMES = (
    "g_mix", "w_in", "w_pool", "pool_b", "pool_scale", "conv_w", "conv_b", "w_gate",
    "gate_a_b", "gate_x_b", "lru_l", "w_out", "g_mlp", "w_up", "w_down", "g_ple",
    "w_ple_gate", "b_ple_gate", "w_ple_proj", "g_final",
)


def _block_kernel(tiles_per_seq, x_ref, p_ref, *rest):
    w = dict(zip(_WEIGHT_NAMES, rest[:len(_WEIGHT_NAMES)]))
    o_ref, pool_halo, conv_halo, h_carry, h1_buf = rest[len(_WEIGHT_NAMES):]
    step = pl.program_id(0)
    t_rows = x_ref.shape[0]

    @pl.when(step == 0)
    def _():
        h1_buf[...] = jnp.zeros_like(h1_buf)

    tile = step % tiles_per_seq
    h1_new, out = _trace_stages(
        _token_mixer(x_ref[...], tile == 0, tile * t_rows, w, pool_halo, conv_halo, h_carry),
        _channel_mixers(h1_buf, p_ref[...], w),
        STAGE_ORDER)
    o_ref[...] = out
    h1_buf[...] = h1_new


def _block_diag(w, group):
    heads, n, _ = w.shape
    w = w.reshape(heads // group, group, n, n)
    eye = jnp.eye(group, dtype=w.dtype)
    out = jnp.einsum('bgij,gh->bgihj', w, eye)
    return out.reshape(heads // group, group * n, group * n)


def _column_blocks(w):
    k, n = w.shape
    return w.reshape(k, n // DOT_N, DOT_N).transpose(1, 0, 2).astype(BF16)


def _resident(shape):
    zeros = (0,) * len(shape)
    return pl.BlockSpec(shape, lambda s: zeros, pipeline_mode=pl.Buffered(1))


@jax.jit
def kernel(x, p, norm_mix_g, w_in, pool_w, pool_b, pool_scale, conv_w, conv_b, gate_a_w, gate_a_b, gate_x_w, gate_x_b, lru_L, w_out, norm_mlp_g, w_up, w_down, norm_ple_g, w_ple_gate, b_ple_gate, w_ple_proj, norm_final_g):
    batch, seq, d_model = x.shape
    assert d_model == D_MODEL and seq % SEQ_TILE == 0
    assert (SEQ_TILE // ROW_BLOCKS) % SUBLANES == 0 and SEQ_TILE // ROW_BLOCKS >= POOL_HALO
    assert p.shape[0] == 1 and w_in.shape[0] == 1, "single-layer block"
    tiles_per_seq = seq // SEQ_TILE
    n_tiles = batch * tiles_per_seq

    row = lambda v: v.reshape(1, -1).astype(F32)
    w_pool = _block_diag(pool_w[0], MXU_DIM // POOL_GROUP_WIDTH).astype(BF16)
    w_gate = jnp.concatenate(
        [_block_diag(gate_a_w[0], MXU_DIM // LRU_BLOCK),
         _block_diag(gate_x_w[0], MXU_DIM // LRU_BLOCK)], axis=-1).astype(BF16)

    weights = dict(
        g_mix=row(norm_mix_g[0]), w_in=_column_blocks(w_in[0]), w_pool=w_pool, pool_b=row(pool_b[0]),
        pool_scale=row(pool_scale[0]), conv_w=conv_w[0].astype(F32), conv_b=row(conv_b[0]),
        w_gate=w_gate, gate_a_b=row(gate_a_b[0]), gate_x_b=row(gate_x_b[0]), lru_l=row(lru_L[0]),
        w_out=_column_blocks(w_out[0]), g_mlp=row(norm_mlp_g[0]), w_up=_column_blocks(w_up[0]),
        w_down=_column_blocks(w_down[0]), g_ple=row(norm_ple_g[0]),
        w_ple_gate=_column_blocks(w_ple_gate[0]), b_ple_gate=row(b_ple_gate[0]),
        w_ple_proj=_column_blocks(w_ple_proj[0]), g_final=row(norm_final_g),
    )
    weight_args = [weights[name] for name in _WEIGHT_NAMES]

    def tile_spec(width, lag):
        def index_map(s):
            t = jnp.clip(s - lag, 0, n_tiles - 1)
            return (t // tiles_per_seq, t % tiles_per_seq, 0)
        return pl.BlockSpec((None, SEQ_TILE, width), index_map)

    return pl.pallas_call(
        functools.partial(_block_kernel, tiles_per_seq),
        grid=(n_tiles + 1,),
        in_specs=[tile_spec(D_MODEL, 0), tile_spec(PLE_DIM, 1)]
                 + [_resident(a.shape) for a in weight_args],
        out_specs=tile_spec(D_MODEL, 1),
        out_shape=jax.ShapeDtypeStruct(x.shape, x.dtype),
        scratch_shapes=[
            pltpu.VMEM((POOL_HALO, D_POOL), F32),
            pltpu.VMEM((CONV_HALO, D_LRU), F32),
            pltpu.VMEM((SUBLANES, D_LRU), F32),
            pltpu.VMEM((SEQ_TILE, D_MODEL), F32),
        ],
        compiler_params=pltpu.CompilerParams(
            dimension_semantics=("arbitrary",),
            vmem_limit_bytes=VMEM_LIMIT_BYTES,
        ),
        name="hybrid_block",
    )(x, p[0], *weight_args)
```

```python
import functools

import jax
import jax.numpy as jnp
from jax import lax
from jax.experimental import pallas as pl
from jax.experimental.pallas import tpu as pltpu

D_MODEL = 1024
D_POOL = 512
D_LRU = 512
POOL_WINDOWS = (2, 4, 8, 16)
POOL_GROUP_WIDTH = D_POOL // len(POOL_WINDOWS)
LRU_HEADS = 8
LRU_BLOCK = D_LRU // LRU_HEADS
CONV_WIDTH = 4
LRU_C = 8.0
D_FF = 4 * D_MODEL
PLE_DIM = 256
RMS_EPS = 1e-6

SUBLANES = 8
MXU_DIM = 256
VMEM_LIMIT_BYTES = 56 * 1024 * 1024

SEQ_TILE = 512
DOT_N = 2 * MXU_DIM
ROW_BLOCKS = 4
POOL_HALO = max(POOL_WINDOWS)
CONV_HALO = SUBLANES
STAGE_ROWS = 512
STAGE_SLOTS = 3
CAST_ROWS = 128

_FF_BLOCKS = D_FF // DOT_N
STAGE_ORDER = ("T" + "C" * (_FF_BLOCKS // 2) + "T" * (2 * ROW_BLOCKS + 2)
               + "C" * (_FF_BLOCKS - _FF_BLOCKS // 2) + "T" * (2 * ROW_BLOCKS)
               + "C" * _FF_BLOCKS + "T" + "C")

F32 = jnp.float32
BF16 = jnp.bfloat16


def _rms_norm(x, g):
    ms = jnp.mean(x * x, axis=-1, keepdims=True)
    return x * lax.rsqrt(ms + RMS_EPS) * g


def _dot(a, w_ref, blocks=None, rows=slice(None)):
    blocks = range(w_ref.shape[0]) if blocks is None else blocks
    outs = [jnp.dot(a, w_ref[j, rows, :], preferred_element_type=F32) for j in blocks]
    return outs[0] if len(outs) == 1 else jnp.concatenate(outs, axis=1)


def _head_dot(a, w_ref):
    return [jnp.dot(a[:, h * MXU_DIM:(h + 1) * MXU_DIM], w_ref[h], preferred_element_type=F32)
            for h in range(w_ref.shape[0])]


def _log_sigmoid(x):
    return -(jnp.maximum(-x, 0.0) + jnp.log1p(jnp.exp(-jnp.abs(x))))


def _linear_recurrence(a, b, h_prev):
    t_rows, width = a.shape
    sub = lax.broadcasted_iota(jnp.int32, (SUBLANES, width), 0)
    groups = []
    for k in range(t_rows // SUBLANES):
        av = a[k * SUBLANES:(k + 1) * SUBLANES]
        bv = b[k * SUBLANES:(k + 1) * SUBLANES]
        shift = 1
        while shift < SUBLANES:
            keep = sub >= shift
            bv = jnp.where(keep, av * pltpu.roll(bv, shift, axis=0) + bv, bv)
            av = jnp.where(keep, av * pltpu.roll(av, shift, axis=0), av)
            shift *= 2
        h = av * h_prev + bv
        groups.append(h)
        h_prev = h[SUBLANES - 1:SUBLANES, :]
    return jnp.concatenate(groups, axis=0), h_prev


def _token_mixer(x, first_tile, row0, w, pool_halo, conv_halo, h_carry):
    t_rows = x.shape[0]
    row_blocks = [slice(k * t_rows // ROW_BLOCKS, (k + 1) * t_rows // ROW_BLOCKS)
                  for k in range(ROW_BLOCKS)]

    @pl.when(first_tile)
    def _():
        pool_halo[...] = jnp.zeros_like(pool_halo)
        conv_halo[...] = jnp.zeros_like(conv_halo)
        h_carry[...] = jnp.zeros_like(h_carry)

    z = _rms_norm(x, w["g_mix"][...]).astype(BF16)
    u_pool = _dot(z, w["w_in"], blocks=[0])
    u_lru = _dot(z, w["w_in"], blocks=[1])
    u_gate = _dot(z, w["w_in"], blocks=[2])
    yield

    t_idx = row0 + lax.broadcasted_iota(jnp.int32, (t_rows, 1), 0)

    ext = jnp.concatenate([pool_halo[...], u_pool], axis=0)
    pool_halo[...] = u_pool[t_rows - POOL_HALO:, :]
    d_blocks = []
    for rows in row_blocks:
        ext_rows = ext[rows.start:rows.stop + POOL_HALO]
        d_groups = []
        for g, win in enumerate(POOL_WINDOWS):
            ug = ext_rows[:, g * POOL_GROUP_WIDTH:(g + 1) * POOL_GROUP_WIDTH]
            s = ug
            k = 1
            while k < win:
                s = s + pltpu.roll(s, k, axis=0)
                k *= 2
            count = jnp.minimum(t_idx[rows] + 1, win).astype(F32)
            d_groups.append(s[POOL_HALO:] / count - ug[POOL_HALO:])
        d_blocks.append(jnp.concatenate(d_groups, axis=1).astype(BF16))
        yield
    d = jnp.concatenate(d_blocks, axis=0)
    y_pool = jnp.concatenate(_head_dot(d, w["w_pool"]), axis=1)
    y_pool = ((y_pool + w["pool_b"][...]) * w["pool_scale"][...]).astype(BF16)
    yield

    ext = jnp.concatenate([conv_halo[...], u_lru], axis=0)
    conv_halo[...] = u_lru[t_rows - CONV_HALO:, :]
    conv_w = w["conv_w"][...]
    xb_blocks = []
    for rows in row_blocks:
        ext_rows = ext[rows.start:rows.stop + CONV_HALO]
        xb = w["conv_b"][...]
        for k in range(CONV_WIDTH):
            lag = CONV_WIDTH - 1 - k
            tap = ext_rows if lag == 0 else pltpu.roll(ext_rows, lag, axis=0)
            xb = xb + tap * conv_w[k:k + 1, :]
        xb_blocks.append(xb[CONV_HALO:])
        yield
    xb = jnp.concatenate(xb_blocks, axis=0)

    gate_parts = _head_dot(xb.astype(BF16), w["w_gate"])
    r_pre = jnp.concatenate([g[:, :MXU_DIM] for g in gate_parts], axis=1)
    i_pre = jnp.concatenate([g[:, MXU_DIM:] for g in gate_parts], axis=1)
    yield
    log_sig_l = _log_sigmoid(w["lru_l"][...])
    a_blocks, b_blocks = [], []
    for rows in row_blocks:
        r = jax.nn.sigmoid(r_pre[rows] + w["gate_a_b"][...])
        i = jax.nn.sigmoid(i_pre[rows] + w["gate_x_b"][...])
        a = jnp.exp(LRU_C * r * log_sig_l)
        mult = jnp.where(t_idx[rows] == 0, 1.0, jnp.sqrt(1.0 - a * a))
        a_blocks.append(a)
        b_blocks.append(mult * (i * xb[rows]))
        yield

    y_blocks, h_last = [], h_carry[0:1, :]
    for k, rows in enumerate(row_blocks):
        h, h_last = _linear_recurrence(a_blocks[k], b_blocks[k], h_last)
        y_blocks.append((h * jax.nn.gelu(u_gate[rows])).astype(BF16))
        yield
    h_carry[0:1, :] = h_last
    y_lru = jnp.concatenate(y_blocks, axis=0)

    return x + (_dot(y_pool, w["w_out"], rows=slice(0, D_POOL))
                + _dot(y_lru, w["w_out"], rows=slice(D_POOL, D_POOL + D_LRU)))


def _channel_mixers(h1_ref, p, w):
    z = _rms_norm(h1_ref[...], w["g_mlp"][...]).astype(BF16)
    acts = []
    for c in range(D_FF // DOT_N):
        up = _dot(z, w["w_up"], blocks=[c])
        acts.append(jnp.square(jnp.maximum(up, 0.0)).astype(BF16))
        yield
    acc = None
    for c, act in enumerate(acts):
        down = _dot(act, w["w_down"], rows=slice(c * DOT_N, (c + 1) * DOT_N))
        acc = down if acc is None else acc + down
        yield
    h2 = h1_ref[...] + acc

    z = _rms_norm(h2, w["g_ple"][...]).astype(BF16)
    gate = jax.nn.sigmoid(_dot(z, w["w_ple_gate"]) + w["b_ple_gate"][...])
    h3 = h2 + gate * _dot(p.astype(BF16), w["w_ple_proj"])
    return _rms_norm(h3, w["g_final"][...])


def _trace_stages(token_stages, channel_stages, order):
    generators = {"T": token_stages, "C": channel_stages}
    results = {}

    def advance(name):
        if name not in results:
            try:
                next(generators[name])
            except StopIteration as stop:
                results[name] = stop.value

    for name in order:
        advance(name)
    for name in generators:
        while name not in results:
            advance(name)
    return results["T"], results["C"]


_VMEM_PARAMS = (
    "g_mix", "w_pool", "pool_b", "pool_scale", "conv_w", "conv_b", "w_gate", "gate_a_b",
    "gate_x_b", "lru_l", "g_mlp", "g_ple", "b_ple_gate", "g_final",
)
_HBM_WEIGHTS = {
    "w_in": (D_MODEL, D_POOL + 2 * D_LRU),
    "w_out": (D_POOL + D_LRU, D_MODEL),
    "w_up": (D_MODEL, D_FF),
    "w_down": (D_FF, D_MODEL),
    "w_ple_gate": (D_MODEL, D_MODEL),
    "w_ple_proj": (PLE_DIM, D_MODEL),
}


def _load_weights(hbm, resident, stage, sems):
    pieces = []
    for name, (k_dim, n_dim) in _HBM_WEIGHTS.items():
        rows = min(k_dim, STAGE_ROWS)
        for j in range(n_dim // DOT_N):
            for r0 in range(0, k_dim, rows):
                pieces.append((hbm[name], r0, j * DOT_N, rows, resident[name], j))

    def copy(i):
        src, r0, c0, rows, _, _ = pieces[i]
        slot = i % STAGE_SLOTS
        return pltpu.make_async_copy(
            src.at[0, pl.ds(r0, rows), pl.ds(c0, DOT_N)],
            stage.at[slot, pl.ds(0, rows), :], sems.at[slot])

    lookahead = STAGE_SLOTS - 1
    for i in range(min(lookahead, len(pieces))):
        copy(i).start()
    for i, (_, r0, _, rows, dst, blk) in enumerate(pieces):
        if i + lookahead < len(pieces):
            copy(i + lookahead).start()
        copy(i).wait()
        slot = i % STAGE_SLOTS

        def cast_rows(k, carry, r0=r0, dst=dst, blk=blk, slot=slot):
            off = pl.multiple_of(k * CAST_ROWS, CAST_ROWS)
            dst[blk, pl.ds(pl.multiple_of(r0 + off, CAST_ROWS), CAST_ROWS), :] = (
                stage[slot, pl.ds(off, CAST_ROWS), :].astype(BF16))
            return carry

        lax.fori_loop(0, rows // CAST_ROWS, cast_rows, 0)


def _block_kernel(tiles_per_seq, x_ref, p_ref, *rest):
    n_vmem, n_hbm = len(_VMEM_PARAMS), len(_HBM_WEIGHTS)
    w = dict(zip(_VMEM_PARAMS, rest[:n_vmem]))
    hbm = dict(zip(_HBM_WEIGHTS, rest[n_vmem:n_vmem + n_hbm]))
    o_ref = rest[n_vmem + n_hbm]
    scratch = rest[n_vmem + n_hbm + 1:]
    resident = dict(zip(_HBM_WEIGHTS, scratch[:n_hbm]))
    stage, sems, pool_halo, conv_halo, h_carry, h1_buf = scratch[n_hbm:]
    w.update(resident)
    step = pl.program_id(0)
    t_rows = x_ref.shape[0]

    @pl.when(step == 0)
    def _():
        _load_weights(hbm, resident, stage, sems)
        h1_buf[...] = jnp.zeros_like(h1_buf)

    tile = step % tiles_per_seq
    h1_new, out = _trace_stages(
        _token_mixer(x_ref[...], tile == 0, tile * t_rows, w, pool_halo, conv_halo, h_carry),
        _channel_mixers(h1_buf, p_ref[...], w),
        STAGE_ORDER)
    o_ref[...] = out
    h1_buf[...] = h1_new


def _block_diag(w, group):
    heads, n, _ = w.shape
    w = w.reshape(heads // group, group, n, n)
    eye = jnp.eye(group, dtype=w.dtype)
    out = jnp.einsum('bgij,gh->bgihj', w, eye)
    return out.reshape(heads // group, group * n, group * n)


def _resident(shape):
    zeros = (0,) * len(shape)
    return pl.BlockSpec(shape, lambda s: zeros, pipeline_mode=pl.Buffered(1))


@jax.jit
def kernel(x, p, norm_mix_g, w_in, pool_w, pool_b, pool_scale, conv_w, conv_b, gate_a_w, gate_a_b, gate_x_w, gate_x_b, lru_L, w_out, norm_mlp_g, w_up, w_down, norm_ple_g, w_ple_gate, b_ple_gate, w_ple_proj, norm_final_g):
    batch, seq, d_model = x.shape
    assert d_model == D_MODEL and seq % SEQ_TILE == 0
    assert (SEQ_TILE // ROW_BLOCKS) % SUBLANES == 0 and SEQ_TILE // ROW_BLOCKS >= POOL_HALO
    assert p.shape[0] == 1 and w_in.shape[0] == 1, "single-layer block"
    tiles_per_seq = seq // SEQ_TILE
    n_tiles = batch * tiles_per_seq

    row = lambda v: v.reshape(1, -1).astype(F32)
    w_pool = _block_diag(pool_w[0], MXU_DIM // POOL_GROUP_WIDTH).astype(BF16)
    w_gate = jnp.concatenate(
        [_block_diag(gate_a_w[0], MXU_DIM // LRU_BLOCK),
         _block_diag(gate_x_w[0], MXU_DIM // LRU_BLOCK)], axis=-1).astype(BF16)

    vmem_params = dict(
        g_mix=row(norm_mix_g[0]), w_pool=w_pool, pool_b=row(pool_b[0]),
        pool_scale=row(pool_scale[0]), conv_w=conv_w[0].astype(F32), conv_b=row(conv_b[0]),
        w_gate=w_gate, gate_a_b=row(gate_a_b[0]), gate_x_b=row(gate_x_b[0]), lru_l=row(lru_L[0]),
        g_mlp=row(norm_mlp_g[0]), g_ple=row(norm_ple_g[0]), b_ple_gate=row(b_ple_gate[0]),
        g_final=row(norm_final_g),
    )
    vmem_args = [vmem_params[name] for name in _VMEM_PARAMS]
    hbm_weights = dict(w_in=w_in, w_out=w_out, w_up=w_up, w_down=w_down,
                       w_ple_gate=w_ple_gate, w_ple_proj=w_ple_proj)
    hbm_args = [hbm_weights[name] for name in _HBM_WEIGHTS]
    for arg, (name, shape) in zip(hbm_args, _HBM_WEIGHTS.items()):
        assert arg.shape == (1,) + shape and arg.dtype == F32, name

    def tile_spec(width, lag):
        def index_map(s):
            t = jnp.clip(s - lag, 0, n_tiles - 1)
            return (t // tiles_per_seq, t % tiles_per_seq, 0)
        return pl.BlockSpec((None, SEQ_TILE, width), index_map)

    return pl.pallas_call(
        functools.partial(_block_kernel, tiles_per_seq),
        grid=(n_tiles + 1,),
        in_specs=[tile_spec(D_MODEL, 0), tile_spec(PLE_DIM, 1)]
                 + [_resident(a.shape) for a in vmem_args]
                 + [pl.BlockSpec(memory_space=pl.ANY)] * len(hbm_args),
        out_specs=tile_spec(D_MODEL, 1),
        out_shape=jax.ShapeDtypeStruct(x.shape, x.dtype),
        scratch_shapes=[pltpu.VMEM((n // DOT_N, k, DOT_N), BF16) for k, n in _HBM_WEIGHTS.values()]
        + [
            pltpu.VMEM((STAGE_SLOTS, STAGE_ROWS, DOT_N), F32),
            pltpu.SemaphoreType.DMA((STAGE_SLOTS,)),
            pltpu.VMEM((POOL_HALO, D_POOL), F32),
            pltpu.VMEM((CONV_HALO, D_LRU), F32),
            pltpu.VMEM((SUBLANES, D_LRU), F32),
            pltpu.VMEM((SEQ_TILE, D_MODEL), F32),
        ],
        compiler_params=pltpu.CompilerParams(
            dimension_semantics=("arbitrary",),
            vmem_limit_bytes=VMEM_LIMIT_BYTES,
        ),
        name="hybrid_block",
    )(x, p[0], *vmem_args, *hbm_args)
```

```python
import functools

import jax
import jax.numpy as jnp
from jax import lax
from jax.experimental import pallas as pl
from jax.experimental.pallas import tpu as pltpu

D_MODEL = 1024
D_POOL = 512
D_LRU = 512
POOL_WINDOWS = (2, 4, 8, 16)
POOL_GROUP_WIDTH = D_POOL // len(POOL_WINDOWS)
LRU_HEADS = 8
LRU_BLOCK = D_LRU // LRU_HEADS
CONV_WIDTH = 4
LRU_C = 8.0
D_FF = 4 * D_MODEL
PLE_DIM = 256
RMS_EPS = 1e-6

SUBLANES = 8
MXU_DIM = 256
VMEM_LIMIT_BYTES = 56 * 1024 * 1024

SEQ_TILE = 512
DOT_N = 2 * MXU_DIM
ROW_BLOCKS = 4
POOL_HALO = max(POOL_WINDOWS)
CONV_HALO = SUBLANES
STAGE_ROWS = 512
STAGE_SLOTS = 3
CAST_ROWS = 128

_FF_BLOCKS = D_FF // DOT_N
STAGE_ORDER = ("T" + "C" * (_FF_BLOCKS // 2) + "T" * (2 * ROW_BLOCKS + 2)
               + "C" * (_FF_BLOCKS - _FF_BLOCKS // 2) + "T" * (2 * ROW_BLOCKS)
               + "C" * _FF_BLOCKS + "T" + "C")

F32 = jnp.float32
BF16 = jnp.bfloat16


def _rms_norm(x, g):
    ms = jnp.mean(x * x, axis=-1, keepdims=True)
    return x * lax.rsqrt(ms + RMS_EPS) * g


def _dot(a, w_ref, blocks=None, rows=slice(None)):
    blocks = range(w_ref.shape[0]) if blocks is None else blocks
    outs = [jnp.dot(a, w_ref[j, rows, :], preferred_element_type=F32) for j in blocks]
    return outs[0] if len(outs) == 1 else jnp.concatenate(outs, axis=1)


def _head_dot(a, w_ref):
    return [jnp.dot(a[:, h * MXU_DIM:(h + 1) * MXU_DIM], w_ref[h], preferred_element_type=F32)
            for h in range(w_ref.shape[0])]


def _log_sigmoid(x):
    return -(jnp.maximum(-x, 0.0) + jnp.log1p(jnp.exp(-jnp.abs(x))))


def _linear_recurrence(a, b, h_prev):
    t_rows, width = a.shape
    sub = lax.broadcasted_iota(jnp.int32, (SUBLANES, width), 0)
    groups = []
    for k in range(t_rows // SUBLANES):
        av = a[k * SUBLANES:(k + 1) * SUBLANES]
        bv = b[k * SUBLANES:(k + 1) * SUBLANES]
        shift = 1
        while shift < SUBLANES:
            keep = sub >= shift
            bv = jnp.where(keep, av * pltpu.roll(bv, shift, axis=0) + bv, bv)
            av = jnp.where(keep, av * pltpu.roll(av, shift, axis=0), av)
            shift *= 2
        h = av * h_prev + bv
        groups.append(h)
        h_prev = h[SUBLANES - 1:SUBLANES, :]
    return jnp.concatenate(groups, axis=0), h_prev


def _token_mixer(x_ref, first_tile, row0, w, pool_halo, conv_halo, h_carry):
    t_rows = x_ref.shape[0]
    row_blocks = [slice(k * t_rows // ROW_BLOCKS, (k + 1) * t_rows // ROW_BLOCKS)
                  for k in range(ROW_BLOCKS)]

    @pl.when(first_tile)
    def _():
        pool_halo[...] = jnp.zeros_like(pool_halo)
        conv_halo[...] = jnp.zeros_like(conv_halo)
        h_carry[...] = jnp.zeros_like(h_carry)

    z = _rms_norm(x_ref[...], w["g_mix"][...]).astype(BF16)
    u_pool = _dot(z, w["w_in"], blocks=[0])
    u_lru = _dot(z, w["w_in"], blocks=[1])
    u_gate = _dot(z, w["w_in"], blocks=[2])
    yield

    t_idx = row0 + lax.broadcasted_iota(jnp.int32, (t_rows, 1), 0)

    ext = jnp.concatenate([pool_halo[...], u_pool], axis=0)
    pool_halo[...] = u_pool[t_rows - POOL_HALO:, :]
    d_blocks = []
    for rows in row_blocks:
        ext_rows = ext[rows.start:rows.stop + POOL_HALO]
        d_groups = []
        for g, win in enumerate(POOL_WINDOWS):
            ug = ext_rows[:, g * POOL_GROUP_WIDTH:(g + 1) * POOL_GROUP_WIDTH]
            s = ug
            k = 1
            while k < win:
                s = s + pltpu.roll(s, k, axis=0)
                k *= 2
            if rows.start < win:
                count = jnp.minimum(t_idx[rows] + 1, win).astype(F32)
                mean = s[POOL_HALO:] / count
            else:
                mean = s[POOL_HALO:] * (1.0 / win)
            d_groups.append(mean - ug[POOL_HALO:])
        d_blocks.append(jnp.concatenate(d_groups, axis=1).astype(BF16))
        yield
    d = jnp.concatenate(d_blocks, axis=0)
    y_pool = jnp.concatenate(_head_dot(d, w["w_pool"]), axis=1)
    y_pool = ((y_pool + w["pool_b"][...]) * w["pool_scale"][...]).astype(BF16)
    yield

    ext = jnp.concatenate([conv_halo[...], u_lru], axis=0)
    conv_halo[...] = u_lru[t_rows - CONV_HALO:, :]
    conv_w = w["conv_w"][...]
    xb_blocks = []
    for rows in row_blocks:
        ext_rows = ext[rows.start:rows.stop + CONV_HALO]
        xb = w["conv_b"][...]
        for k in range(CONV_WIDTH):
            lag = CONV_WIDTH - 1 - k
            tap = ext_rows if lag == 0 else pltpu.roll(ext_rows, lag, axis=0)
            xb = xb + tap * conv_w[k:k + 1, :]
        xb_blocks.append(xb[CONV_HALO:])
        yield
    xb = jnp.concatenate(xb_blocks, axis=0)

    gate_parts = _head_dot(xb.astype(BF16), w["w_gate"])
    r_pre = jnp.concatenate([g[:, :MXU_DIM] for g in gate_parts], axis=1)
    i_pre = jnp.concatenate([g[:, MXU_DIM:] for g in gate_parts], axis=1)
    yield
    log_sig_l = _log_sigmoid(w["lru_l"][...])
    a_blocks, b_blocks = [], []
    for rows in row_blocks:
        r = jax.nn.sigmoid(r_pre[rows] + w["gate_a_b"][...])
        i = jax.nn.sigmoid(i_pre[rows] + w["gate_x_b"][...])
        a = jnp.exp(LRU_C * r * log_sig_l)
        one_minus_a2 = 1.0 - a * a
        mult = jnp.where(one_minus_a2 > 0.0, one_minus_a2 * lax.rsqrt(one_minus_a2), 0.0)
        mult = jnp.where(t_idx[rows] == 0, 1.0, mult)
        a_blocks.append(a)
        b_blocks.append(mult * (i * xb[rows]))
        yield

    y_blocks, h_last = [], h_carry[0:1, :]
    for k, rows in enumerate(row_blocks):
        h, h_last = _linear_recurrence(a_blocks[k], b_blocks[k], h_last)
        y_blocks.append((h * jax.nn.gelu(u_gate[rows])).astype(BF16))
        yield
    h_carry[0:1, :] = h_last
    y_lru = jnp.concatenate(y_blocks, axis=0)

    return x_ref[...] + (_dot(y_pool, w["w_out"], rows=slice(0, D_POOL))
                         + _dot(y_lru, w["w_out"], rows=slice(D_POOL, D_POOL + D_LRU)))


def _channel_mixers(h1_ref, p_ref, w):
    z = _rms_norm(h1_ref[...], w["g_mlp"][...]).astype(BF16)
    acts = []
    for c in range(D_FF // DOT_N):
        up = _dot(z, w["w_up"], blocks=[c])
        act = jnp.maximum(up.astype(BF16), 0.0)
        acts.append(act * act)
        yield
    acc = None
    for c, act in enumerate(acts):
        down = _dot(act, w["w_down"], rows=slice(c * DOT_N, (c + 1) * DOT_N))
        acc = down if acc is None else acc + down
        yield
    h2 = h1_ref[...] + acc

    z = _rms_norm(h2, w["g_ple"][...]).astype(BF16)
    gate = jax.nn.sigmoid(_dot(z, w["w_ple_gate"]) + w["b_ple_gate"][...])
    h3 = h2 + gate * _dot(p_ref[...].astype(BF16), w["w_ple_proj"])
    return _rms_norm(h3, w["g_final"][...])


def _trace_stages(token_stages, channel_stages, order):
    generators = {"T": token_stages, "C": channel_stages}
    results = {}

    def advance(name):
        if name not in results:
            try:
                next(generators[name])
            except StopIteration as stop:
                results[name] = stop.value

    for name in order:
        advance(name)
    for name in generators:
        while name not in results:
            advance(name)
    return results["T"], results["C"]


_VMEM_PARAMS = (
    "g_mix", "w_pool", "pool_b", "pool_scale", "conv_w", "conv_b", "w_gate", "gate_a_b",
    "gate_x_b", "lru_l", "g_mlp", "g_ple", "b_ple_gate", "g_final",
)
_HBM_WEIGHTS = {
    "w_in": (D_MODEL, D_POOL + 2 * D_LRU),
    "w_out": (D_POOL + D_LRU, D_MODEL),
    "w_up": (D_MODEL, D_FF),
    "w_down": (D_FF, D_MODEL),
    "w_ple_gate": (D_MODEL, D_MODEL),
    "w_ple_proj": (PLE_DIM, D_MODEL),
}


def _load_weights(hbm, resident, stage, sems):
    pieces = []
    for name, (k_dim, n_dim) in _HBM_WEIGHTS.items():
        rows = min(k_dim, STAGE_ROWS)
        for j in range(n_dim // DOT_N):
            for r0 in range(0, k_dim, rows):
                pieces.append((hbm[name], r0, j * DOT_N, rows, resident[name], j))

    def copy(i):
        src, r0, c0, rows, _, _ = pieces[i]
        slot = i % STAGE_SLOTS
        return pltpu.make_async_copy(
            src.at[0, pl.ds(r0, rows), pl.ds(c0, DOT_N)],
            stage.at[slot, pl.ds(0, rows), :], sems.at[slot])

    lookahead = STAGE_SLOTS - 1
    for i in range(min(lookahead, len(pieces))):
        copy(i).start()
    for i, (_, r0, _, rows, dst, blk) in enumerate(pieces):
        if i + lookahead < len(pieces):
            copy(i + lookahead).start()
        copy(i).wait()
        slot = i % STAGE_SLOTS

        def cast_rows(k, carry, r0=r0, dst=dst, blk=blk, slot=slot):
            off = pl.multiple_of(k * CAST_ROWS, CAST_ROWS)
            dst[blk, pl.ds(pl.multiple_of(r0 + off, CAST_ROWS), CAST_ROWS), :] = (
                stage[slot, pl.ds(off, CAST_ROWS), :].astype(BF16))
            return carry

        lax.fori_loop(0, rows // CAST_ROWS, cast_rows, 0)


def _block_kernel(tiles_per_seq, x_ref, p_ref, *rest):
    n_vmem, n_hbm = len(_VMEM_PARAMS), len(_HBM_WEIGHTS)
    w = dict(zip(_VMEM_PARAMS, rest[:n_vmem]))
    hbm = dict(zip(_HBM_WEIGHTS, rest[n_vmem:n_vmem + n_hbm]))
    o_ref = rest[n_vmem + n_hbm]
    scratch = rest[n_vmem + n_hbm + 1:]
    resident = dict(zip(_HBM_WEIGHTS, scratch[:n_hbm]))
    stage, sems, pool_halo, conv_halo, h_carry, h1_buf = scratch[n_hbm:]
    w.update(resident)
    step = pl.program_id(0)
    t_rows = x_ref.shape[0]

    @pl.when(step == 0)
    def _():
        _load_weights(hbm, resident, stage, sems)
        h1_buf[...] = jnp.zeros_like(h1_buf)

    tile = step % tiles_per_seq
    h1_new, out = _trace_stages(
        _token_mixer(x_ref, tile == 0, tile * t_rows, w, pool_halo, conv_halo, h_carry),
        _channel_mixers(h1_buf, p_ref, w),
        STAGE_ORDER)
    o_ref[...] = out
    h1_buf[...] = h1_new


def _block_diag(w, group):
    heads, n, _ = w.shape
    w = w.reshape(heads // group, group, n, n)
    eye = jnp.eye(group, dtype=w.dtype)
    out = jnp.einsum('bgij,gh->bgihj', w, eye)
    return out.reshape(heads // group, group * n, group * n)


def _resident(shape):
    zeros = (0,) * len(shape)
    return pl.BlockSpec(shape, lambda s: zeros, pipeline_mode=pl.Buffered(1))


@jax.jit
def kernel(x, p, norm_mix_g, w_in, pool_w, pool_b, pool_scale, conv_w, conv_b, gate_a_w, gate_a_b, gate_x_w, gate_x_b, lru_L, w_out, norm_mlp_g, w_up, w_down, norm_ple_g, w_ple_gate, b_ple_gate, w_ple_proj, norm_final_g):
    batch, seq, d_model = x.shape
    assert d_model == D_MODEL and seq % SEQ_TILE == 0
    assert (SEQ_TILE // ROW_BLOCKS) % SUBLANES == 0 and SEQ_TILE // ROW_BLOCKS >= POOL_HALO
    assert p.shape[0] == 1 and w_in.shape[0] == 1, "single-layer block"
    tiles_per_seq = seq // SEQ_TILE
    n_tiles = batch * tiles_per_seq

    row = lambda v: v.reshape(1, -1).astype(F32)
    w_pool = _block_diag(pool_w[0], MXU_DIM // POOL_GROUP_WIDTH).astype(BF16)
    w_gate = jnp.concatenate(
        [_block_diag(gate_a_w[0], MXU_DIM // LRU_BLOCK),
         _block_diag(gate_x_w[0], MXU_DIM // LRU_BLOCK)], axis=-1).astype(BF16)

    vmem_params = dict(
        g_mix=row(norm_mix_g[0]), w_pool=w_pool, pool_b=row(pool_b[0]),
        pool_scale=row(pool_scale[0]), conv_w=conv_w[0].astype(F32), conv_b=row(conv_b[0]),
        w_gate=w_gate, gate_a_b=row(gate_a_b[0]), gate_x_b=row(gate_x_b[0]), lru_l=row(lru_L[0]),
        g_mlp=row(norm_mlp_g[0]), g_ple=row(norm_ple_g[0]), b_ple_gate=row(b_ple_gate[0]),
        g_final=row(norm_final_g),
    )
    vmem_args = [vmem_params[name] for name in _VMEM_PARAMS]
    hbm_weights = dict(w_in=w_in, w_out=w_out, w_up=w_up, w_down=w_down,
                       w_ple_gate=w_ple_gate, w_ple_proj=w_ple_proj)
    hbm_args = [hbm_weights[name] for name in _HBM_WEIGHTS]
    for arg, (name, shape) in zip(hbm_args, _HBM_WEIGHTS.items()):
        assert arg.shape == (1,) + shape and arg.dtype == F32, name

    def tile_spec(width, lag):
        def index_map(s):
            t = jnp.clip(s - lag, 0, n_tiles - 1)
            return (t // tiles_per_seq, t % tiles_per_seq, 0)
        return pl.BlockSpec((None, SEQ_TILE, width), index_map)

    return pl.pallas_call(
        functools.partial(_block_kernel, tiles_per_seq),
        grid=(n_tiles + 1,),
        in_specs=[tile_spec(D_MODEL, 0), tile_spec(PLE_DIM, 1)]
                 + [_resident(a.shape) for a in vmem_args]
                 + [pl.BlockSpec(memory_space=pl.ANY)] * len(hbm_args),
        out_specs=tile_spec(D_MODEL, 1),
        out_shape=jax.ShapeDtypeStruct(x.shape, x.dtype),
        scratch_shapes=[pltpu.VMEM((n // DOT_N, k, DOT_N), BF16) for k, n in _HBM_WEIGHTS.values()]
        + [
            pltpu.VMEM((STAGE_SLOTS, STAGE_ROWS, DOT_N), F32),
            pltpu.SemaphoreType.DMA((STAGE_SLOTS,)),
            pltpu.VMEM((POOL_HALO, D_POOL), F32),
            pltpu.VMEM((CONV_HALO, D_LRU), F32),
            pltpu.VMEM((SUBLANES, D_LRU), F32),
            pltpu.VMEM((SEQ_TILE, D_MODEL), F32),
        ],
        compiler_params=pltpu.CompilerParams(
            dimension_semantics=("arbitrary",),
            vmem_limit_bytes=VMEM_LIMIT_BYTES,
        ),
        name="hybrid_block",
    )(x, p[0], *vmem_args, *hbm_args)
```

```python
import functools

import jax
import jax.numpy as jnp
from jax import lax
from jax.experimental import pallas as pl
from jax.experimental.pallas import tpu as pltpu

D_MODEL = 1024
D_POOL = 512
D_LRU = 512
POOL_WINDOWS = (2, 4, 8, 16)
POOL_GROUP_WIDTH = D_POOL // len(POOL_WINDOWS)
LRU_HEADS = 8
LRU_BLOCK = D_LRU // LRU_HEADS
CONV_WIDTH = 4
LRU_C = 8.0
D_FF = 4 * D_MODEL
PLE_DIM = 256
RMS_EPS = 1e-6

SUBLANES = 8
MXU_DIM = 256
VMEM_LIMIT_BYTES = 56 * 1024 * 1024

SEQ_TILE = 512
DOT_N = 2 * MXU_DIM
ROW_BLOCKS = 4
PLE_ROW_BLOCKS = 2
POOL_HALO = max(POOL_WINDOWS)
CONV_HALO = SUBLANES
STAGE_ROWS = 512
STAGE_SLOTS = 3
CAST_ROWS = 128

_FF_BLOCKS = D_FF // DOT_N
STAGE_ORDER = ("T" + "C" * (_FF_BLOCKS // 2) + "T" * (2 * ROW_BLOCKS + 2)
               + "C" * (_FF_BLOCKS - _FF_BLOCKS // 2) + "T" * (2 * ROW_BLOCKS)
               + "C" * _FF_BLOCKS + "T" + "C")

F32 = jnp.float32
BF16 = jnp.bfloat16


def _rms_norm(x, g):
    ms = jnp.mean(x * x, axis=-1, keepdims=True)
    return x * lax.rsqrt(ms + RMS_EPS) * g


def _dot(a, w_ref, blocks=None, rows=slice(None)):
    blocks = range(w_ref.shape[0]) if blocks is None else blocks
    outs = [jnp.dot(a, w_ref[j, rows, :], preferred_element_type=F32) for j in blocks]
    return outs[0] if len(outs) == 1 else jnp.concatenate(outs, axis=1)


def _head_dot(a, w_ref):
    return [jnp.dot(a[:, h * MXU_DIM:(h + 1) * MXU_DIM], w_ref[h], preferred_element_type=F32)
            for h in range(w_ref.shape[0])]


def _sigmoid(x):
    return 0.5 * jnp.tanh(0.5 * x) + 0.5


def _log_sigmoid(x):
    return -(jnp.maximum(-x, 0.0) + jnp.log1p(jnp.exp(-jnp.abs(x))))


def _linear_recurrence(a, b, h_prev):
    t_rows, width = a.shape
    sub = lax.broadcasted_iota(jnp.int32, (SUBLANES, width), 0)
    groups = []
    for k in range(t_rows // SUBLANES):
        av = a[k * SUBLANES:(k + 1) * SUBLANES]
        bv = b[k * SUBLANES:(k + 1) * SUBLANES]
        shift = 1
        while shift < SUBLANES:
            keep = sub >= shift
            bv = jnp.where(keep, av * pltpu.roll(bv, shift, axis=0) + bv, bv)
            av = jnp.where(keep, av * pltpu.roll(av, shift, axis=0), av)
            shift *= 2
        h = av * h_prev + bv
        groups.append(h)
        h_prev = h[SUBLANES - 1:SUBLANES, :]
    return jnp.concatenate(groups, axis=0), h_prev


def _token_mixer(x_ref, first_tile, row0, w, pool_halo, conv_halo, h_carry):
    t_rows = x_ref.shape[0]
    row_blocks = [slice(k * t_rows // ROW_BLOCKS, (k + 1) * t_rows // ROW_BLOCKS)
                  for k in range(ROW_BLOCKS)]

    @pl.when(first_tile)
    def _():
        pool_halo[...] = jnp.zeros_like(pool_halo)
        conv_halo[...] = jnp.zeros_like(conv_halo)
        h_carry[...] = jnp.zeros_like(h_carry)

    z = _rms_norm(x_ref[...], w["g_mix"][...]).astype(BF16)
    u_pool = _dot(z, w["w_in"], blocks=[0])
    u_lru = _dot(z, w["w_in"], blocks=[1])
    u_gate = _dot(z, w["w_in"], blocks=[2])
    yield

    t_idx = row0 + lax.broadcasted_iota(jnp.int32, (t_rows, 1), 0)

    ext = jnp.concatenate([pool_halo[...], u_pool], axis=0)
    pool_halo[...] = u_pool[t_rows - POOL_HALO:, :]
    d_blocks = []
    for rows in row_blocks:
        ext_rows = ext[rows.start:rows.stop + POOL_HALO]
        d_groups = []
        for g, win in enumerate(POOL_WINDOWS):
            ug = ext_rows[:, g * POOL_GROUP_WIDTH:(g + 1) * POOL_GROUP_WIDTH]
            s = ug
            k = 1
            while k < win:
                s = s + pltpu.roll(s, k, axis=0)
                k *= 2
            if rows.start < win:
                count = jnp.minimum(t_idx[rows] + 1, win).astype(F32)
                mean = s[POOL_HALO:] / count
            else:
                mean = s[POOL_HALO:] * (1.0 / win)
            d_groups.append(mean - ug[POOL_HALO:])
        d_blocks.append(jnp.concatenate(d_groups, axis=1).astype(BF16))
        yield
    d = jnp.concatenate(d_blocks, axis=0)
    y_pool = jnp.concatenate(_head_dot(d, w["w_pool"]), axis=1)
    y_pool = ((y_pool + w["pool_b"][...]) * w["pool_scale"][...]).astype(BF16)
    yield

    ext = jnp.concatenate([conv_halo[...], u_lru], axis=0)
    conv_halo[...] = u_lru[t_rows - CONV_HALO:, :]
    conv_w = w["conv_w"][...]
    xb_blocks = []
    for rows in row_blocks:
        ext_rows = ext[rows.start:rows.stop + CONV_HALO]
        xb = w["conv_b"][...]
        for k in range(CONV_WIDTH):
            lag = CONV_WIDTH - 1 - k
            tap = ext_rows if lag == 0 else pltpu.roll(ext_rows, lag, axis=0)
            xb = xb + tap * conv_w[k:k + 1, :]
        xb_blocks.append(xb[CONV_HALO:])
        yield
    xb = jnp.concatenate(xb_blocks, axis=0)

    gate_parts = _head_dot(xb.astype(BF16), w["w_gate"])
    r_pre = jnp.concatenate([g[:, :MXU_DIM] for g in gate_parts], axis=1)
    i_pre = jnp.concatenate([g[:, MXU_DIM:] for g in gate_parts], axis=1)
    yield
    log_sig_l = _log_sigmoid(w["lru_l"][...])
    a_blocks, b_blocks = [], []
    for rows in row_blocks:
        r = jax.nn.sigmoid(r_pre[rows] + w["gate_a_b"][...])
        i = jax.nn.sigmoid(i_pre[rows] + w["gate_x_b"][...])
        a = jnp.exp(LRU_C * r * log_sig_l)
        one_minus_a2 = 1.0 - a * a
        mult = jnp.where(one_minus_a2 > 0.0, one_minus_a2 * lax.rsqrt(one_minus_a2), 0.0)
        mult = jnp.where(t_idx[rows] == 0, 1.0, mult)
        a_blocks.append(a)
        b_blocks.append(mult * (i * xb[rows]))
        yield

    y_blocks, h_last = [], h_carry[0:1, :]
    for k, rows in enumerate(row_blocks):
        h, h_last = _linear_recurrence(a_blocks[k], b_blocks[k], h_last)
        y_blocks.append((h * jax.nn.gelu(u_gate[rows])).astype(BF16))
        yield
    h_carry[0:1, :] = h_last
    y_lru = jnp.concatenate(y_blocks, axis=0)

    return x_ref[...] + (_dot(y_pool, w["w_out"], rows=slice(0, D_POOL))
                         + _dot(y_lru, w["w_out"], rows=slice(D_POOL, D_POOL + D_LRU)))


def _channel_mixers(h1_ref, p_ref, w):
    z = _rms_norm(h1_ref[...], w["g_mlp"][...]).astype(BF16)
    acts = []
    for c in range(D_FF // DOT_N):
        up = _dot(z, w["w_up"], blocks=[c])
        act = jnp.maximum(up.astype(BF16), 0.0)
        acts.append(act * act)
        yield
    acc = None
    for c, act in enumerate(acts):
        down = _dot(act, w["w_down"], rows=slice(c * DOT_N, (c + 1) * DOT_N))
        acc = down if acc is None else acc + down
        yield
    h2 = h1_ref[...] + acc

    z = _rms_norm(h2, w["g_ple"][...]).astype(BF16)
    ple = _dot(p_ref[...].astype(BF16), w["w_ple_proj"])
    t_rows = z.shape[0]
    outs = []
    for k in range(PLE_ROW_BLOCKS):
        rows = slice(k * t_rows // PLE_ROW_BLOCKS, (k + 1) * t_rows // PLE_ROW_BLOCKS)
        gate = _sigmoid(_dot(z[rows], w["w_ple_gate"]) + w["b_ple_gate"][...])
        h3 = h2[rows] + gate * ple[rows]
        outs.append(_rms_norm(h3, w["g_final"][...]))
    return jnp.concatenate(outs, axis=0)


def _trace_stages(token_stages, channel_stages, order):
    generators = {"T": token_stages, "C": channel_stages}
    results = {}

    def advance(name):
        if name not in results:
            try:
                next(generators[name])
            except StopIteration as stop:
                results[name] = stop.value

    for name in order:
        advance(name)
    for name in generators:
        while name not in results:
            advance(name)
    return results["T"], results["C"]


_VMEM_PARAMS = (
    "g_mix", "w_pool", "pool_b", "pool_scale", "conv_w", "conv_b", "w_gate", "gate_a_b",
    "gate_x_b", "lru_l", "g_mlp", "g_ple", "b_ple_gate", "g_final",
)
_HBM_WEIGHTS = {
    "w_in": (D_MODEL, D_POOL + 2 * D_LRU),
    "w_out": (D_POOL + D_LRU, D_MODEL),
    "w_up": (D_MODEL, D_FF),
    "w_down": (D_FF, D_MODEL),
    "w_ple_gate": (D_MODEL, D_MODEL),
    "w_ple_proj": (PLE_DIM, D_MODEL),
}


def _load_weights(hbm, resident, stage, sems):
    pieces = []
    for name, (k_dim, n_dim) in _HBM_WEIGHTS.items():
        rows = min(k_dim, STAGE_ROWS)
        for j in range(n_dim // DOT_N):
            for r0 in range(0, k_dim, rows):
                pieces.append((hbm[name], r0, j * DOT_N, rows, resident[name], j))

    def copy(i):
        src, r0, c0, rows, _, _ = pieces[i]
        slot = i % STAGE_SLOTS
        return pltpu.make_async_copy(
            src.at[0, pl.ds(r0, rows), pl.ds(c0, DOT_N)],
            stage.at[slot, pl.ds(0, rows), :], sems.at[slot])

    lookahead = STAGE_SLOTS - 1
    for i in range(min(lookahead, len(pieces))):
        copy(i).start()
    for i, (_, r0, _, rows, dst, blk) in enumerate(pieces):
        if i + lookahead < len(pieces):
            copy(i + lookahead).start()
        copy(i).wait()
        slot = i % STAGE_SLOTS

        def cast_rows(k, carry, r0=r0, dst=dst, blk=blk, slot=slot):
            off = pl.multiple_of(k * CAST_ROWS, CAST_ROWS)
            dst[blk, pl.ds(pl.multiple_of(r0 + off, CAST_ROWS), CAST_ROWS), :] = (
                stage[slot, pl.ds(off, CAST_ROWS), :].astype(BF16))
            return carry

        lax.fori_loop(0, rows // CAST_ROWS, cast_rows, 0)


def _block_kernel(tiles_per_seq, x_ref, p_ref, *rest):
    n_vmem, n_hbm = len(_VMEM_PARAMS), len(_HBM_WEIGHTS)
    w = dict(zip(_VMEM_PARAMS, rest[:n_vmem]))
    hbm = dict(zip(_HBM_WEIGHTS, rest[n_vmem:n_vmem + n_hbm]))
    o_ref = rest[n_vmem + n_hbm]
    scratch = rest[n_vmem + n_hbm + 1:]
    resident = dict(zip(_HBM_WEIGHTS, scratch[:n_hbm]))
    stage, sems, pool_halo, conv_halo, h_carry, h1_buf = scratch[n_hbm:]
    w.update(resident)
    step = pl.program_id(0)
    t_rows = x_ref.shape[0]

    tile = step % tiles_per_seq

    def token_mixer():
        return _token_mixer(x_ref, tile == 0, tile * t_rows, w, pool_halo, conv_halo, h_carry)

    @pl.when(step == 0)
    def _():
        _load_weights(hbm, resident, stage, sems)
        h1_new, _ = _trace_stages(token_mixer(), iter(()), "")
        h1_buf[...] = h1_new

    @pl.when(step > 0)
    def _():
        h1_new, out = _trace_stages(token_mixer(), _channel_mixers(h1_buf, p_ref, w), STAGE_ORDER)
        o_ref[...] = out
        h1_buf[...] = h1_new


def _block_diag(w, group):
    heads, n, _ = w.shape
    w = w.reshape(heads // group, group, n, n)
    eye = jnp.eye(group, dtype=w.dtype)
    out = jnp.einsum('bgij,gh->bgihj', w, eye)
    return out.reshape(heads // group, group * n, group * n)


def _resident(shape):
    zeros = (0,) * len(shape)
    return pl.BlockSpec(shape, lambda s: zeros, pipeline_mode=pl.Buffered(1))


@jax.jit
def kernel(x, p, norm_mix_g, w_in, pool_w, pool_b, pool_scale, conv_w, conv_b, gate_a_w, gate_a_b, gate_x_w, gate_x_b, lru_L, w_out, norm_mlp_g, w_up, w_down, norm_ple_g, w_ple_gate, b_ple_gate, w_ple_proj, norm_final_g):
    batch, seq, d_model = x.shape
    assert d_model == D_MODEL and seq % SEQ_TILE == 0
    assert (SEQ_TILE // ROW_BLOCKS) % SUBLANES == 0 and SEQ_TILE // ROW_BLOCKS >= POOL_HALO
    assert p.shape[0] == 1 and w_in.shape[0] == 1, "single-layer block"
    tiles_per_seq = seq // SEQ_TILE
    n_tiles = batch * tiles_per_seq

    row = lambda v: v.reshape(1, -1).astype(F32)
    w_pool = _block_diag(pool_w[0], MXU_DIM // POOL_GROUP_WIDTH).astype(BF16)
    w_gate = jnp.concatenate(
        [_block_diag(gate_a_w[0], MXU_DIM // LRU_BLOCK),
         _block_diag(gate_x_w[0], MXU_DIM // LRU_BLOCK)], axis=-1).astype(BF16)

    vmem_params = dict(
        g_mix=row(norm_mix_g[0]), w_pool=w_pool, pool_b=row(pool_b[0]),
        pool_scale=row(pool_scale[0]), conv_w=conv_w[0].astype(F32), conv_b=row(conv_b[0]),
        w_gate=w_gate, gate_a_b=row(gate_a_b[0]), gate_x_b=row(gate_x_b[0]), lru_l=row(lru_L[0]),
        g_mlp=row(norm_mlp_g[0]), g_ple=row(norm_ple_g[0]), b_ple_gate=row(b_ple_gate[0]),
        g_final=row(norm_final_g),
    )
    vmem_args = [vmem_params[name] for name in _VMEM_PARAMS]
    hbm_weights = dict(w_in=w_in, w_out=w_out, w_up=w_up, w_down=w_down,
                       w_ple_gate=w_ple_gate, w_ple_proj=w_ple_proj)
    hbm_args = [hbm_weights[name] for name in _HBM_WEIGHTS]
    for arg, (name, shape) in zip(hbm_args, _HBM_WEIGHTS.items()):
        assert arg.shape == (1,) + shape and arg.dtype == F32, name

    def tile_spec(width, lag):
        def index_map(s):
            t = jnp.clip(s - lag, 0, n_tiles - 1)
            return (t // tiles_per_seq, t % tiles_per_seq, 0)
        return pl.BlockSpec((None, SEQ_TILE, width), index_map)

    return pl.pallas_call(
        functools.partial(_block_kernel, tiles_per_seq),
        grid=(n_tiles + 1,),
        in_specs=[tile_spec(D_MODEL, 0), tile_spec(PLE_DIM, 1)]
                 + [_resident(a.shape) for a in vmem_args]
                 + [pl.BlockSpec(memory_space=pl.ANY)] * len(hbm_args),
        out_specs=tile_spec(D_MODEL, 1),
        out_shape=jax.ShapeDtypeStruct(x.shape, x.dtype),
        scratch_shapes=[pltpu.VMEM((n // DOT_N, k, DOT_N), BF16) for k, n in _HBM_WEIGHTS.values()]
        + [
            pltpu.VMEM((STAGE_SLOTS, STAGE_ROWS, DOT_N), F32),
            pltpu.SemaphoreType.DMA((STAGE_SLOTS,)),
            pltpu.VMEM((POOL_HALO, D_POOL), F32),
            pltpu.VMEM((CONV_HALO, D_LRU), F32),
            pltpu.VMEM((SUBLANES, D_LRU), F32),
            pltpu.VMEM((SEQ_TILE, D_MODEL), F32),
        ],
        compiler_params=pltpu.CompilerParams(
            dimension_semantics=("arbitrary",),
            vmem_limit_bytes=VMEM_LIMIT_BYTES,
        ),
        name="hybrid_block",
    )(x, p[0], *vmem_args, *hbm_args)
```

```python
import functools

import jax
import jax.numpy as jnp
from jax import lax
from jax.experimental import pallas as pl
from jax.experimental.pallas import tpu as pltpu

D_MODEL = 1024
D_POOL = 512
D_LRU = 512
POOL_WINDOWS = (2, 4, 8, 16)
POOL_GROUP_WIDTH = D_POOL // len(POOL_WINDOWS)
LRU_HEADS = 8
LRU_BLOCK = D_LRU // LRU_HEADS
CONV_WIDTH = 4
LRU_C = 8.0
D_FF = 4 * D_MODEL
PLE_DIM = 256
RMS_EPS = 1e-6

SUBLANES = 8
MXU_DIM = 256
VMEM_LIMIT_BYTES = 56 * 1024 * 1024

SEQ_TILE = 512
DOT_N = 2 * MXU_DIM
ROW_BLOCKS = 4
PLE_ROW_BLOCKS = 2
POOL_HALO = max(POOL_WINDOWS)
CONV_HALO = SUBLANES
STAGE_ROWS = 512
STAGE_SLOTS = 3
CAST_ROWS = 128

_FF_BLOCKS = D_FF // DOT_N
assert _FF_BLOCKS - _FF_BLOCKS // 2 == ROW_BLOCKS
STAGE_ORDER = ("T" + "C" * (_FF_BLOCKS // 2) + "T" * (2 * ROW_BLOCKS + 1)
               + "TTC" * ROW_BLOCKS + "C" * _FF_BLOCKS + "T" + "C")

F32 = jnp.float32
BF16 = jnp.bfloat16


def _rms_norm(x, g):
    ms = jnp.mean(x * x, axis=-1, keepdims=True)
    return x * lax.rsqrt(ms + RMS_EPS) * g


def _dot(a, w_ref, blocks=None, rows=slice(None)):
    blocks = range(w_ref.shape[0]) if blocks is None else blocks
    outs = [jnp.dot(a, w_ref[j, rows, :], preferred_element_type=F32) for j in blocks]
    return outs[0] if len(outs) == 1 else jnp.concatenate(outs, axis=1)


def _head_dot(a, w_ref):
    return [jnp.dot(a[:, h * MXU_DIM:(h + 1) * MXU_DIM], w_ref[h], preferred_element_type=F32)
            for h in range(w_ref.shape[0])]


def _sigmoid(x):
    return 0.5 * jnp.tanh(0.5 * x) + 0.5


def _log_sigmoid(x):
    return -(jnp.maximum(-x, 0.0) + jnp.log1p(jnp.exp(-jnp.abs(x))))


def _linear_recurrence(a, b, h_prev):
    t_rows, width = a.shape
    sub = lax.broadcasted_iota(jnp.int32, (SUBLANES, width), 0)
    groups = []
    for k in range(t_rows // SUBLANES):
        av = a[k * SUBLANES:(k + 1) * SUBLANES]
        bv = b[k * SUBLANES:(k + 1) * SUBLANES]
        shift = 1
        while shift < SUBLANES:
            keep = sub >= shift
            bv = jnp.where(keep, av * pltpu.roll(bv, shift, axis=0) + bv, bv)
            av = jnp.where(keep, av * pltpu.roll(av, shift, axis=0), av)
            shift *= 2
        h = av * h_prev + bv
        groups.append(h)
        h_prev = h[SUBLANES - 1:SUBLANES, :]
    return jnp.concatenate(groups, axis=0), h_prev


def _token_mixer(x_ref, first_tile, row0, w, pool_halo, conv_halo, h_carry):
    t_rows = x_ref.shape[0]
    row_blocks = [slice(k * t_rows // ROW_BLOCKS, (k + 1) * t_rows // ROW_BLOCKS)
                  for k in range(ROW_BLOCKS)]

    @pl.when(first_tile)
    def _():
        pool_halo[...] = jnp.zeros_like(pool_halo)
        conv_halo[...] = jnp.zeros_like(conv_halo)
        h_carry[...] = jnp.zeros_like(h_carry)

    z = _rms_norm(x_ref[...], w["g_mix"][...]).astype(BF16)
    u_pool = _dot(z, w["w_in"], blocks=[0])
    u_lru = _dot(z, w["w_in"], blocks=[1])
    u_gate = _dot(z, w["w_in"], blocks=[2])
    yield

    t_idx = row0 + lax.broadcasted_iota(jnp.int32, (t_rows, 1), 0)

    ext = jnp.concatenate([pool_halo[...], u_pool], axis=0)
    pool_halo[...] = u_pool[t_rows - POOL_HALO:, :]
    d_blocks = []
    for rows in row_blocks:
        ext_rows = ext[rows.start:rows.stop + POOL_HALO]
        d_groups = []
        for g, win in enumerate(POOL_WINDOWS):
            ug = ext_rows[:, g * POOL_GROUP_WIDTH:(g + 1) * POOL_GROUP_WIDTH]
            s = ug
            k = 1
            while k < win:
                s = s + pltpu.roll(s, k, axis=0)
                k *= 2
            if rows.start < win:
                count = jnp.minimum(t_idx[rows] + 1, win).astype(F32)
                mean = s[POOL_HALO:] / count
            else:
                mean = s[POOL_HALO:] * (1.0 / win)
            d_groups.append(mean - ug[POOL_HALO:])
        d_blocks.append(jnp.concatenate(d_groups, axis=1).astype(BF16))
        yield
    d = jnp.concatenate(d_blocks, axis=0)
    y_pool = jnp.concatenate(_head_dot(d, w["w_pool"]), axis=1)
    y_pool = ((y_pool + w["pool_b"][...]) * w["pool_scale"][...]).astype(BF16)
    yield

    ext = jnp.concatenate([conv_halo[...], u_lru], axis=0)
    conv_halo[...] = u_lru[t_rows - CONV_HALO:, :]
    conv_w = w["conv_w"][...]
    xb_blocks = []
    for rows in row_blocks:
        ext_rows = ext[rows.start:rows.stop + CONV_HALO]
        xb = w["conv_b"][...]
        for k in range(CONV_WIDTH):
            lag = CONV_WIDTH - 1 - k
            tap = ext_rows if lag == 0 else pltpu.roll(ext_rows, lag, axis=0)
            xb = xb + tap * conv_w[k:k + 1, :]
        xb_blocks.append(xb[CONV_HALO:])
        yield
    xb = jnp.concatenate(xb_blocks, axis=0)

    xb_bf = xb.astype(BF16)
    log_sig_l = _log_sigmoid(w["lru_l"][...])
    y_blocks, h_last = [], h_carry[0:1, :]
    for rows in row_blocks:
        gate_parts = _head_dot(xb_bf[rows], w["w_gate"])
        r_pre = jnp.concatenate([g[:, :MXU_DIM] for g in gate_parts], axis=1)
        i_pre = jnp.concatenate([g[:, MXU_DIM:] for g in gate_parts], axis=1)
        r = jax.nn.sigmoid(r_pre + w["gate_a_b"][...])
        i = jax.nn.sigmoid(i_pre + w["gate_x_b"][...])
        a = jnp.exp(LRU_C * r * log_sig_l)
        one_minus_a2 = 1.0 - a * a
        mult = jnp.where(one_minus_a2 > 0.0, one_minus_a2 * lax.rsqrt(one_minus_a2), 0.0)
        mult = jnp.where(t_idx[rows] == 0, 1.0, mult)
        b = mult * (i * xb[rows])
        yield
        h, h_last = _linear_recurrence(a, b, h_last)
        y_blocks.append((h * jax.nn.gelu(u_gate[rows])).astype(BF16))
        yield
    h_carry[0:1, :] = h_last
    y_lru = jnp.concatenate(y_blocks, axis=0)

    return x_ref[...] + (_dot(y_pool, w["w_out"], rows=slice(0, D_POOL))
                         + _dot(y_lru, w["w_out"], rows=slice(D_POOL, D_POOL + D_LRU)))


def _channel_mixers(h1_ref, p_ref, w):
    z = _rms_norm(h1_ref[...], w["g_mlp"][...]).astype(BF16)
    acts = []
    for c in range(D_FF // DOT_N):
        up = _dot(z, w["w_up"], blocks=[c])
        act = jnp.maximum(up.astype(BF16), 0.0)
        acts.append(act * act)
        yield
    acc = None
    for c, act in enumerate(acts):
        down = _dot(act, w["w_down"], rows=slice(c * DOT_N, (c + 1) * DOT_N))
        acc = down if acc is None else acc + down
        yield
    h2 = h1_ref[...] + acc

    z = _rms_norm(h2, w["g_ple"][...]).astype(BF16)
    ple = _dot(p_ref[...].astype(BF16), w["w_ple_proj"])
    t_rows = z.shape[0]
    outs = []
    for k in range(PLE_ROW_BLOCKS):
        rows = slice(k * t_rows // PLE_ROW_BLOCKS, (k + 1) * t_rows // PLE_ROW_BLOCKS)
        gate = _sigmoid(_dot(z[rows], w["w_ple_gate"]) + w["b_ple_gate"][...])
        h3 = h2[rows] + gate * ple[rows]
        outs.append(_rms_norm(h3, w["g_final"][...]))
    return jnp.concatenate(outs, axis=0)


def _trace_stages(token_stages, channel_stages, order):
    generators = {"T": token_stages, "C": channel_stages}
    results = {}

    def advance(name):
        if name not in results:
            try:
                next(generators[name])
            except StopIteration as stop:
                results[name] = stop.value

    for name in order:
        advance(name)
    for name in generators:
        while name not in results:
            advance(name)
    return results["T"], results["C"]


_VMEM_PARAMS = (
    "g_mix", "w_pool", "pool_b", "pool_scale", "conv_w", "conv_b", "w_gate", "gate_a_b",
    "gate_x_b", "lru_l", "g_mlp", "g_ple", "b_ple_gate", "g_final",
)
_HBM_WEIGHTS = {
    "w_in": (D_MODEL, D_POOL + 2 * D_LRU),
    "w_out": (D_POOL + D_LRU, D_MODEL),
    "w_up": (D_MODEL, D_FF),
    "w_down": (D_FF, D_MODEL),
    "w_ple_gate": (D_MODEL, D_MODEL),
    "w_ple_proj": (PLE_DIM, D_MODEL),
}


def _load_weights(hbm, resident, stage, sems):
    pieces = []
    for name, (k_dim, n_dim) in _HBM_WEIGHTS.items():
        rows = min(k_dim, STAGE_ROWS)
        for j in range(n_dim // DOT_N):
            for r0 in range(0, k_dim, rows):
                pieces.append((hbm[name], r0, j * DOT_N, rows, resident[name], j))

    def copy(i):
        src, r0, c0, rows, _, _ = pieces[i]
        slot = i % STAGE_SLOTS
        return pltpu.make_async_copy(
            src.at[0, pl.ds(r0, rows), pl.ds(c0, DOT_N)],
            stage.at[slot, pl.ds(0, rows), :], sems.at[slot])

    lookahead = STAGE_SLOTS - 1
    for i in range(min(lookahead, len(pieces))):
        copy(i).start()
    for i, (_, r0, _, rows, dst, blk) in enumerate(pieces):
        if i + lookahead < len(pieces):
            copy(i + lookahead).start()
        copy(i).wait()
        slot = i % STAGE_SLOTS

        def cast_rows(k, carry, r0=r0, dst=dst, blk=blk, slot=slot):
            off = pl.multiple_of(k * CAST_ROWS, CAST_ROWS)
            dst[blk, pl.ds(pl.multiple_of(r0 + off, CAST_ROWS), CAST_ROWS), :] = (
                stage[slot, pl.ds(off, CAST_ROWS), :].astype(BF16))
            return carry

        lax.fori_loop(0, rows // CAST_ROWS, cast_rows, 0)


def _block_kernel(tiles_per_seq, x_ref, p_ref, *rest):
    n_vmem, n_hbm = len(_VMEM_PARAMS), len(_HBM_WEIGHTS)
    w = dict(zip(_VMEM_PARAMS, rest[:n_vmem]))
    hbm = dict(zip(_HBM_WEIGHTS, rest[n_vmem:n_vmem + n_hbm]))
    o_ref = rest[n_vmem + n_hbm]
    scratch = rest[n_vmem + n_hbm + 1:]
    resident = dict(zip(_HBM_WEIGHTS, scratch[:n_hbm]))
    stage, sems, pool_halo, conv_halo, h_carry, h1_buf = scratch[n_hbm:]
    w.update(resident)
    step = pl.program_id(0)
    t_rows = x_ref.shape[0]

    tile = step % tiles_per_seq

    def token_mixer():
        return _token_mixer(x_ref, tile == 0, tile * t_rows, w, pool_halo, conv_halo, h_carry)

    @pl.when(step == 0)
    def _():
        _load_weights(hbm, resident, stage, sems)
        h1_new, _ = _trace_stages(token_mixer(), iter(()), "")
        h1_buf[...] = h1_new

    @pl.when(step > 0)
    def _():
        h1_new, out = _trace_stages(token_mixer(), _channel_mixers(h1_buf, p_ref, w), STAGE_ORDER)
        o_ref[...] = out
        h1_buf[...] = h1_new


def _block_diag(w, group):
    heads, n, _ = w.shape
    w = w.reshape(heads // group, group, n, n)
    eye = jnp.eye(group, dtype=w.dtype)
    out = jnp.einsum('bgij,gh->bgihj', w, eye)
    return out.reshape(heads // group, group * n, group * n)


def _resident(shape):
    zeros = (0,) * len(shape)
    return pl.BlockSpec(shape, lambda s: zeros, pipeline_mode=pl.Buffered(1))


@jax.jit
def kernel(x, p, norm_mix_g, w_in, pool_w, pool_b, pool_scale, conv_w, conv_b, gate_a_w, gate_a_b, gate_x_w, gate_x_b, lru_L, w_out, norm_mlp_g, w_up, w_down, norm_ple_g, w_ple_gate, b_ple_gate, w_ple_proj, norm_final_g):
    batch, seq, d_model = x.shape
    assert d_model == D_MODEL and seq % SEQ_TILE == 0
    assert (SEQ_TILE // ROW_BLOCKS) % SUBLANES == 0 and SEQ_TILE // ROW_BLOCKS >= POOL_HALO
    assert p.shape[0] == 1 and w_in.shape[0] == 1, "single-layer block"
    tiles_per_seq = seq // SEQ_TILE
    n_tiles = batch * tiles_per_seq

    row = lambda v: v.reshape(1, -1).astype(F32)
    w_pool = _block_diag(pool_w[0], MXU_DIM // POOL_GROUP_WIDTH).astype(BF16)
    w_gate = jnp.concatenate(
        [_block_diag(gate_a_w[0], MXU_DIM // LRU_BLOCK),
         _block_diag(gate_x_w[0], MXU_DIM // LRU_BLOCK)], axis=-1).astype(BF16)

    vmem_params = dict(
        g_mix=row(norm_mix_g[0]), w_pool=w_pool, pool_b=row(pool_b[0]),
        pool_scale=row(pool_scale[0]), conv_w=conv_w[0].astype(F32), conv_b=row(conv_b[0]),
        w_gate=w_gate, gate_a_b=row(gate_a_b[0]), gate_x_b=row(gate_x_b[0]), lru_l=row(lru_L[0]),
        g_mlp=row(norm_mlp_g[0]), g_ple=row(norm_ple_g[0]), b_ple_gate=row(b_ple_gate[0]),
        g_final=row(norm_final_g),
    )
    vmem_args = [vmem_params[name] for name in _VMEM_PARAMS]
    hbm_weights = dict(w_in=w_in, w_out=w_out, w_up=w_up, w_down=w_down,
                       w_ple_gate=w_ple_gate, w_ple_proj=w_ple_proj)
    hbm_args = [hbm_weights[name] for name in _HBM_WEIGHTS]
    for arg, (name, shape) in zip(hbm_args, _HBM_WEIGHTS.items()):
        assert arg.shape == (1,) + shape and arg.dtype == F32, name

    def tile_spec(width, lag):
        def index_map(s):
            t = jnp.clip(s - lag, 0, n_tiles - 1)
            return (t // tiles_per_seq, t % tiles_per_seq, 0)
        return pl.BlockSpec((None, SEQ_TILE, width), index_map)

    return pl.pallas_call(
        functools.partial(_block_kernel, tiles_per_seq),
        grid=(n_tiles + 1,),
        in_specs=[tile_spec(D_MODEL, 0), tile_spec(PLE_DIM, 1)]
                 + [_resident(a.shape) for a in vmem_args]
                 + [pl.BlockSpec(memory_space=pl.ANY)] * len(hbm_args),
        out_specs=tile_spec(D_MODEL, 1),
        out_shape=jax.ShapeDtypeStruct(x.shape, x.dtype),
        scratch_shapes=[pltpu.VMEM((n // DOT_N, k, DOT_N), BF16) for k, n in _HBM_WEIGHTS.values()]
        + [
            pltpu.VMEM((STAGE_SLOTS, STAGE_ROWS, DOT_N), F32),
            pltpu.SemaphoreType.DMA((STAGE_SLOTS,)),
            pltpu.VMEM((POOL_HALO, D_POOL), F32),
            pltpu.VMEM((CONV_HALO, D_LRU), F32),
            pltpu.VMEM((SUBLANES, D_LRU), F32),
            pltpu.VMEM((SEQ_TILE, D_MODEL), F32),
        ],
        compiler_params=pltpu.CompilerParams(
            dimension_semantics=("arbitrary",),
            vmem_limit_bytes=VMEM_LIMIT_BYTES,
        ),
        name="hybrid_block",
    )(x, p[0], *vmem_args, *hbm_args)
```

```python
import functools

import jax
import jax.numpy as jnp
from jax import lax
from jax.experimental import pallas as pl
from jax.experimental.pallas import tpu as pltpu

D_MODEL = 1024
D_POOL = 512
D_LRU = 512
POOL_WINDOWS = (2, 4, 8, 16)
POOL_GROUP_WIDTH = D_POOL // len(POOL_WINDOWS)
LRU_HEADS = 8
LRU_BLOCK = D_LRU // LRU_HEADS
CONV_WIDTH = 4
LRU_C = 8.0
D_FF = 4 * D_MODEL
PLE_DIM = 256
RMS_EPS = 1e-6

SUBLANES = 8
MXU_DIM = 256
VMEM_LIMIT_BYTES = 56 * 1024 * 1024

SEQ_TILE = 512
DOT_N = 2 * MXU_DIM
ROW_BLOCKS = 4
PLE_ROW_BLOCKS = 2
POOL_HALO = max(POOL_WINDOWS)
CONV_HALO = SUBLANES
STAGE_ROWS = 512
STAGE_SLOTS = 3
CAST_ROWS = 128

_FF_BLOCKS = D_FF // DOT_N
assert _FF_BLOCKS - _FF_BLOCKS // 2 == ROW_BLOCKS
STAGE_ORDER = ("T" + "C" * (_FF_BLOCKS // 2) + "T" * (2 * ROW_BLOCKS + 1)
               + "TTC" * ROW_BLOCKS + "C" * _FF_BLOCKS + "T" + "CC" + "T")

F32 = jnp.float32
BF16 = jnp.bfloat16


def _rms_norm(x, g):
    ms = jnp.mean(x * x, axis=-1, keepdims=True)
    return x * lax.rsqrt(ms + RMS_EPS) * g


def _dot(a, w_ref, blocks=None, rows=slice(None)):
    blocks = range(w_ref.shape[0]) if blocks is None else blocks
    outs = [jnp.dot(a, w_ref[j, rows, :], preferred_element_type=F32) for j in blocks]
    return outs[0] if len(outs) == 1 else jnp.concatenate(outs, axis=1)


def _head_dot(a, w_ref):
    return [jnp.dot(a[:, h * MXU_DIM:(h + 1) * MXU_DIM], w_ref[h], preferred_element_type=F32)
            for h in range(w_ref.shape[0])]


def _sigmoid(x):
    return 0.5 * jnp.tanh(0.5 * x) + 0.5


def _log_sigmoid(x):
    return -(jnp.maximum(-x, 0.0) + jnp.log1p(jnp.exp(-jnp.abs(x))))


def _linear_recurrence(a, b, h_prev):
    t_rows, width = a.shape
    sub = lax.broadcasted_iota(jnp.int32, (SUBLANES, width), 0)
    groups = []
    for k in range(t_rows // SUBLANES):
        av = a[k * SUBLANES:(k + 1) * SUBLANES]
        bv = b[k * SUBLANES:(k + 1) * SUBLANES]
        shift = 1
        while shift < SUBLANES:
            keep = sub >= shift
            bv = jnp.where(keep, av * pltpu.roll(bv, shift, axis=0) + bv, bv)
            av = jnp.where(keep, av * pltpu.roll(av, shift, axis=0), av)
            shift *= 2
        h = av * h_prev + bv
        groups.append(h)
        h_prev = h[SUBLANES - 1:SUBLANES, :]
    return jnp.concatenate(groups, axis=0), h_prev


def _token_mixer(x_ref, first_tile, row0, w, pool_halo, conv_halo, h_carry):
    t_rows = x_ref.shape[0]
    row_blocks = [slice(k * t_rows // ROW_BLOCKS, (k + 1) * t_rows // ROW_BLOCKS)
                  for k in range(ROW_BLOCKS)]

    @pl.when(first_tile)
    def _():
        pool_halo[...] = jnp.zeros_like(pool_halo)
        conv_halo[...] = jnp.zeros_like(conv_halo)
        h_carry[...] = jnp.zeros_like(h_carry)

    z = _rms_norm(x_ref[...], w["g_mix"][...]).astype(BF16)
    u_pool = _dot(z, w["w_in"], blocks=[0])
    u_lru = _dot(z, w["w_in"], blocks=[1])
    u_gate = _dot(z, w["w_in"], blocks=[2])
    yield

    t_idx = row0 + lax.broadcasted_iota(jnp.int32, (t_rows, 1), 0)

    ext = jnp.concatenate([pool_halo[...], u_pool], axis=0)
    pool_halo[...] = u_pool[t_rows - POOL_HALO:, :]
    d_blocks = []
    for rows in row_blocks:
        ext_rows = ext[rows.start:rows.stop + POOL_HALO]
        d_groups = []
        for g, win in enumerate(POOL_WINDOWS):
            ug = ext_rows[:, g * POOL_GROUP_WIDTH:(g + 1) * POOL_GROUP_WIDTH]
            s = ug
            k = 1
            while k < win:
                s = s + pltpu.roll(s, k, axis=0)
                k *= 2
            if rows.start < win:
                count = jnp.minimum(t_idx[rows] + 1, win).astype(F32)
                mean = s[POOL_HALO:] / count
            else:
                mean = s[POOL_HALO:] * (1.0 / win)
            d_groups.append(mean - ug[POOL_HALO:])
        d_blocks.append(jnp.concatenate(d_groups, axis=1).astype(BF16))
        yield
    d = jnp.concatenate(d_blocks, axis=0)
    y_pool = jnp.concatenate(_head_dot(d, w["w_pool"]), axis=1)
    y_pool = ((y_pool + w["pool_b"][...]) * w["pool_scale"][...]).astype(BF16)
    yield

    ext = jnp.concatenate([conv_halo[...], u_lru], axis=0)
    conv_halo[...] = u_lru[t_rows - CONV_HALO:, :]
    conv_w = w["conv_w"][...]
    xb_blocks = []
    for rows in row_blocks:
        ext_rows = ext[rows.start:rows.stop + CONV_HALO]
        xb = w["conv_b"][...]
        for k in range(CONV_WIDTH):
            lag = CONV_WIDTH - 1 - k
            tap = ext_rows if lag == 0 else pltpu.roll(ext_rows, lag, axis=0)
            xb = xb + tap * conv_w[k:k + 1, :]
        xb_blocks.append(xb[CONV_HALO:])
        yield
    xb = jnp.concatenate(xb_blocks, axis=0)

    xb_bf = xb.astype(BF16)
    log_sig_l = _log_sigmoid(w["lru_l"][...])
    y_blocks, h_last = [], h_carry[0:1, :]
    for rows in row_blocks:
        gate_parts = _head_dot(xb_bf[rows], w["w_gate"])
        r_pre = jnp.concatenate([g[:, :MXU_DIM] for g in gate_parts], axis=1)
        i_pre = jnp.concatenate([g[:, MXU_DIM:] for g in gate_parts], axis=1)
        r = jax.nn.sigmoid(r_pre + w["gate_a_b"][...])
        i = jax.nn.sigmoid(i_pre + w["gate_x_b"][...])
        a = jnp.exp(LRU_C * r * log_sig_l)
        one_minus_a2 = 1.0 - a * a
        mult = jnp.where(one_minus_a2 > 0.0, one_minus_a2 * lax.rsqrt(one_minus_a2), 0.0)
        mult = jnp.where(t_idx[rows] == 0, 1.0, mult)
        b = mult * (i * xb[rows])
        yield
        h, h_last = _linear_recurrence(a, b, h_last)
        y_blocks.append((h * jax.nn.gelu(u_gate[rows])).astype(BF16))
        yield
    h_carry[0:1, :] = h_last
    y_lru = jnp.concatenate(y_blocks, axis=0)

    mixed = []
    for j in range(D_MODEL // DOT_N):
        if j:
            yield
        mixed.append(_dot(y_pool, w["w_out"], blocks=[j], rows=slice(0, D_POOL))
                     + _dot(y_lru, w["w_out"], blocks=[j], rows=slice(D_POOL, D_POOL + D_LRU)))
    return x_ref[...] + jnp.concatenate(mixed, axis=1)


def _channel_mixers(h1_ref, p_ref, w):
    z = _rms_norm(h1_ref[...], w["g_mlp"][...]).astype(BF16)
    acts = []
    for c in range(D_FF // DOT_N):
        up = _dot(z, w["w_up"], blocks=[c])
        act = jnp.maximum(up.astype(BF16), 0.0)
        acts.append(act * act)
        yield
    acc = None
    for c, act in enumerate(acts):
        down = _dot(act, w["w_down"], rows=slice(c * DOT_N, (c + 1) * DOT_N))
        acc = down if acc is None else acc + down
        yield
    ple = _dot(p_ref[...].astype(BF16), w["w_ple_proj"])
    yield
    h2 = h1_ref[...] + acc
    z = _rms_norm(h2, w["g_ple"][...]).astype(BF16)
    t_rows = z.shape[0]
    outs = []
    for k in range(PLE_ROW_BLOCKS):
        rows = slice(k * t_rows // PLE_ROW_BLOCKS, (k + 1) * t_rows // PLE_ROW_BLOCKS)
        gate = _sigmoid(_dot(z[rows], w["w_ple_gate"]) + w["b_ple_gate"][...])
        h3 = h2[rows] + gate * ple[rows]
        outs.append(_rms_norm(h3, w["g_final"][...]))
    return jnp.concatenate(outs, axis=0)


def _trace_stages(token_stages, channel_stages, order):
    generators = {"T": token_stages, "C": channel_stages}
    results = {}

    def advance(name):
        if name not in results:
            try:
                next(generators[name])
            except StopIteration as stop:
                results[name] = stop.value

    for name in order:
        advance(name)
    for name in generators:
        while name not in results:
            advance(name)
    return results["T"], results["C"]


_VMEM_PARAMS = (
    "g_mix", "w_pool", "pool_b", "pool_scale", "conv_w", "conv_b", "w_gate", "gate_a_b",
    "gate_x_b", "lru_l", "g_mlp", "g_ple", "b_ple_gate", "g_final",
)
_HBM_WEIGHTS = {
    "w_in": (D_MODEL, D_POOL + 2 * D_LRU),
    "w_out": (D_POOL + D_LRU, D_MODEL),
    "w_up": (D_MODEL, D_FF),
    "w_down": (D_FF, D_MODEL),
    "w_ple_gate": (D_MODEL, D_MODEL),
    "w_ple_proj": (PLE_DIM, D_MODEL),
}


def _load_weights(hbm, resident, stage, sems):
    pieces = []
    for name, (k_dim, n_dim) in _HBM_WEIGHTS.items():
        rows = min(k_dim, STAGE_ROWS)
        for j in range(n_dim // DOT_N):
            for r0 in range(0, k_dim, rows):
                pieces.append((hbm[name], r0, j * DOT_N, rows, resident[name], j))

    def copy(i):
        src, r0, c0, rows, _, _ = pieces[i]
        slot = i % STAGE_SLOTS
        return pltpu.make_async_copy(
            src.at[0, pl.ds(r0, rows), pl.ds(c0, DOT_N)],
            stage.at[slot, pl.ds(0, rows), :], sems.at[slot])

    lookahead = STAGE_SLOTS - 1
    for i in range(min(lookahead, len(pieces))):
        copy(i).start()
    for i, (_, r0, _, rows, dst, blk) in enumerate(pieces):
        if i + lookahead < len(pieces):
            copy(i + lookahead).start()
        copy(i).wait()
        slot = i % STAGE_SLOTS

        def cast_rows(k, carry, r0=r0, dst=dst, blk=blk, slot=slot):
            off = pl.multiple_of(k * CAST_ROWS, CAST_ROWS)
            dst[blk, pl.ds(pl.multiple_of(r0 + off, CAST_ROWS), CAST_ROWS), :] = (
                stage[slot, pl.ds(off, CAST_ROWS), :].astype(BF16))
            return carry

        lax.fori_loop(0, rows // CAST_ROWS, cast_rows, 0)


def _block_kernel(tiles_per_seq, x_ref, p_ref, *rest):
    n_vmem, n_hbm = len(_VMEM_PARAMS), len(_HBM_WEIGHTS)
    w = dict(zip(_VMEM_PARAMS, rest[:n_vmem]))
    hbm = dict(zip(_HBM_WEIGHTS, rest[n_vmem:n_vmem + n_hbm]))
    o_ref = rest[n_vmem + n_hbm]
    scratch = rest[n_vmem + n_hbm + 1:]
    resident = dict(zip(_HBM_WEIGHTS, scratch[:n_hbm]))
    stage, sems, pool_halo, conv_halo, h_carry, h1_buf = scratch[n_hbm:]
    w.update(resident)
    step = pl.program_id(0)
    t_rows = x_ref.shape[0]

    tile = step % tiles_per_seq

    def token_mixer():
        return _token_mixer(x_ref, tile == 0, tile * t_rows, w, pool_halo, conv_halo, h_carry)

    @pl.when(step == 0)
    def _():
        _load_weights(hbm, resident, stage, sems)
        h1_new, _ = _trace_stages(token_mixer(), iter(()), "")
        h1_buf[...] = h1_new

    @pl.when(step > 0)
    def _():
        h1_new, out = _trace_stages(token_mixer(), _channel_mixers(h1_buf, p_ref, w), STAGE_ORDER)
        o_ref[...] = out
        h1_buf[...] = h1_new


def _block_diag(w, group):
    heads, n, _ = w.shape
    w = w.reshape(heads // group, group, n, n)
    eye = jnp.eye(group, dtype=w.dtype)
    out = jnp.einsum('bgij,gh->bgihj', w, eye)
    return out.reshape(heads // group, group * n, group * n)


def _resident(shape):
    zeros = (0,) * len(shape)
    return pl.BlockSpec(shape, lambda s: zeros, pipeline_mode=pl.Buffered(1))


@jax.jit
def kernel(x, p, norm_mix_g, w_in, pool_w, pool_b, pool_scale, conv_w, conv_b, gate_a_w, gate_a_b, gate_x_w, gate_x_b, lru_L, w_out, norm_mlp_g, w_up, w_down, norm_ple_g, w_ple_gate, b_ple_gate, w_ple_proj, norm_final_g):
    batch, seq, d_model = x.shape
    assert d_model == D_MODEL and seq % SEQ_TILE == 0
    assert (SEQ_TILE // ROW_BLOCKS) % SUBLANES == 0 and SEQ_TILE // ROW_BLOCKS >= POOL_HALO
    assert p.shape[0] == 1 and w_in.shape[0] == 1, "single-layer block"
    tiles_per_seq = seq // SEQ_TILE
    n_tiles = batch * tiles_per_seq

    row = lambda v: v.reshape(1, -1).astype(F32)
    w_pool = _block_diag(pool_w[0], MXU_DIM // POOL_GROUP_WIDTH).astype(BF16)
    w_gate = jnp.concatenate(
        [_block_diag(gate_a_w[0], MXU_DIM // LRU_BLOCK),
         _block_diag(gate_x_w[0], MXU_DIM // LRU_BLOCK)], axis=-1).astype(BF16)

    vmem_params = dict(
        g_mix=row(norm_mix_g[0]), w_pool=w_pool, pool_b=row(pool_b[0]),
        pool_scale=row(pool_scale[0]), conv_w=conv_w[0].astype(F32), conv_b=row(conv_b[0]),
        w_gate=w_gate, gate_a_b=row(gate_a_b[0]), gate_x_b=row(gate_x_b[0]), lru_l=row(lru_L[0]),
        g_mlp=row(norm_mlp_g[0]), g_ple=row(norm_ple_g[0]), b_ple_gate=row(b_ple_gate[0]),
        g_final=row(norm_final_g),
    )
    vmem_args = [vmem_params[name] for name in _VMEM_PARAMS]
    hbm_weights = dict(w_in=w_in, w_out=w_out, w_up=w_up, w_down=w_down,
                       w_ple_gate=w_ple_gate, w_ple_proj=w_ple_proj)
    hbm_args = [hbm_weights[name] for name in _HBM_WEIGHTS]
    for arg, (name, shape) in zip(hbm_args, _HBM_WEIGHTS.items()):
        assert arg.shape == (1,) + shape and arg.dtype == F32, name

    def tile_spec(width, lag):
        def index_map(s):
            t = jnp.clip(s - lag, 0, n_tiles - 1)
            return (t // tiles_per_seq, t % tiles_per_seq, 0)
        return pl.BlockSpec((None, SEQ_TILE, width), index_map)

    return pl.pallas_call(
        functools.partial(_block_kernel, tiles_per_seq),
        grid=(n_tiles + 1,),
        in_specs=[tile_spec(D_MODEL, 0), tile_spec(PLE_DIM, 1)]
                 + [_resident(a.shape) for a in vmem_args]
                 + [pl.BlockSpec(memory_space=pl.ANY)] * len(hbm_args),
        out_specs=tile_spec(D_MODEL, 1),
        out_shape=jax.ShapeDtypeStruct(x.shape, x.dtype),
        scratch_shapes=[pltpu.VMEM((n // DOT_N, k, DOT_N), BF16) for k, n in _HBM_WEIGHTS.values()]
        + [
            pltpu.VMEM((STAGE_SLOTS, STAGE_ROWS, DOT_N), F32),
            pltpu.SemaphoreType.DMA((STAGE_SLOTS,)),
            pltpu.VMEM((POOL_HALO, D_POOL), F32),
            pltpu.VMEM((CONV_HALO, D_LRU), F32),
            pltpu.VMEM((SUBLANES, D_LRU), F32),
            pltpu.VMEM((SEQ_TILE, D_MODEL), F32),
        ],
        compiler_params=pltpu.CompilerParams(
            dimension_semantics=("arbitrary",),
            vmem_limit_bytes=VMEM_LIMIT_BYTES,
        ),
        name="hybrid_block",
    )(x, p[0], *vmem_args, *hbm_args)
```

```python
import functools

import jax
import jax.numpy as jnp
from jax import lax
from jax.experimental import pallas as pl
from jax.experimental.pallas import tpu as pltpu

D_MODEL = 1024
D_POOL = 512
D_LRU = 512
POOL_WINDOWS = (2, 4, 8, 16)
POOL_GROUP_WIDTH = D_POOL // len(POOL_WINDOWS)
LRU_HEADS = 8
LRU_BLOCK = D_LRU // LRU_HEADS
CONV_WIDTH = 4
LRU_C = 8.0
D_FF = 4 * D_MODEL
PLE_DIM = 256
RMS_EPS = 1e-6

SUBLANES = 8
MXU_DIM = 256
VMEM_LIMIT_BYTES = 56 * 1024 * 1024

SEQ_TILE = 512
DOT_N = 2 * MXU_DIM
ROW_BLOCKS = 4
PLE_ROW_BLOCKS = 2
POOL_HALO = max(POOL_WINDOWS)
CONV_HALO = SUBLANES
STAGE_ROWS = 512
STAGE_SLOTS = 3
CAST_ROWS = 128

_FF_BLOCKS = D_FF // DOT_N
assert _FF_BLOCKS - _FF_BLOCKS // 2 == ROW_BLOCKS
STAGE_ORDER = ("T" + "C" * (_FF_BLOCKS // 2) + "T" * (2 * ROW_BLOCKS + 1)
               + "TTC" * ROW_BLOCKS + "C" * _FF_BLOCKS + "T" + "C")

F32 = jnp.float32
BF16 = jnp.bfloat16


def _rms_norm(x, g):
    ms = jnp.mean(x * x, axis=-1, keepdims=True)
    return x * lax.rsqrt(ms + RMS_EPS) * g


def _dot(a, w_ref, blocks=None, rows=slice(None)):
    blocks = range(w_ref.shape[0]) if blocks is None else blocks
    outs = [jnp.dot(a, w_ref[j, rows, :], preferred_element_type=F32) for j in blocks]
    return outs[0] if len(outs) == 1 else jnp.concatenate(outs, axis=1)


def _head_dot(a, w_ref):
    return [jnp.dot(a[:, h * MXU_DIM:(h + 1) * MXU_DIM], w_ref[h], preferred_element_type=F32)
            for h in range(w_ref.shape[0])]


def _sigmoid(x):
    return 0.5 * jnp.tanh(0.5 * x) + 0.5


def _log_sigmoid(x):
    return -(jnp.maximum(-x, 0.0) + jnp.log1p(jnp.exp(-jnp.abs(x))))


def _linear_recurrence(a, b, h_prev):
    t_rows, width = a.shape
    sub = lax.broadcasted_iota(jnp.int32, (SUBLANES, width), 0)
    groups = []
    for k in range(t_rows // SUBLANES):
        av = a[k * SUBLANES:(k + 1) * SUBLANES]
        bv = b[k * SUBLANES:(k + 1) * SUBLANES]
        shift = 1
        while shift < SUBLANES:
            keep = sub >= shift
            bv = jnp.where(keep, av * pltpu.roll(bv, shift, axis=0) + bv, bv)
            av = jnp.where(keep, av * pltpu.roll(av, shift, axis=0), av)
            shift *= 2
        h = av * h_prev + bv
        groups.append(h)
        h_prev = h[SUBLANES - 1:SUBLANES, :]
    return jnp.concatenate(groups, axis=0), h_prev


def _token_mixer(x_ref, first_tile, row0, w, pool_halo, conv_halo, h_carry):
    t_rows = x_ref.shape[0]
    row_blocks = [slice(k * t_rows // ROW_BLOCKS, (k + 1) * t_rows // ROW_BLOCKS)
                  for k in range(ROW_BLOCKS)]

    @pl.when(first_tile)
    def _():
        pool_halo[...] = jnp.zeros_like(pool_halo)
        conv_halo[...] = jnp.zeros_like(conv_halo)
        h_carry[...] = jnp.zeros_like(h_carry)

    z = _rms_norm(x_ref[...], w["g_mix"][...]).astype(BF16)
    u_pool = _dot(z, w["w_in"], blocks=[0])
    u_lru = _dot(z, w["w_in"], blocks=[1])
    u_gate = _dot(z, w["w_in"], blocks=[2])
    yield

    t_idx = row0 + lax.broadcasted_iota(jnp.int32, (t_rows, 1), 0)

    ext = jnp.concatenate([pool_halo[...], u_pool], axis=0)
    pool_halo[...] = u_pool[t_rows - POOL_HALO:, :]
    d_blocks = []
    for rows in row_blocks:
        ext_rows = ext[rows.start:rows.stop + POOL_HALO]
        d_groups = []
        for g, win in enumerate(POOL_WINDOWS):
            ug = ext_rows[:, g * POOL_GROUP_WIDTH:(g + 1) * POOL_GROUP_WIDTH]
            s = ug
            k = 1
            while k < win:
                s = s + pltpu.roll(s, k, axis=0)
                k *= 2
            if rows.start < win:
                count = jnp.minimum(t_idx[rows] + 1, win).astype(F32)
                mean = s[POOL_HALO:] / count
            else:
                mean = s[POOL_HALO:] * (1.0 / win)
            d_groups.append(mean - ug[POOL_HALO:])
        d_blocks.append(jnp.concatenate(d_groups, axis=1).astype(BF16))
        yield
    d = jnp.concatenate(d_blocks, axis=0)
    y_pool = jnp.concatenate(_head_dot(d, w["w_pool"]), axis=1)
    y_pool = ((y_pool + w["pool_b"][...]) * w["pool_scale"][...]).astype(BF16)
    yield

    ext = jnp.concatenate([conv_halo[...], u_lru], axis=0)
    conv_halo[...] = u_lru[t_rows - CONV_HALO:, :]
    conv_w = w["conv_w"][...]
    xb_blocks = []
    for rows in row_blocks:
        ext_rows = ext[rows.start:rows.stop + CONV_HALO]
        xb = w["conv_b"][...]
        for k in range(CONV_WIDTH):
            lag = CONV_WIDTH - 1 - k
            tap = ext_rows if lag == 0 else pltpu.roll(ext_rows, lag, axis=0)
            xb = xb + tap * conv_w[k:k + 1, :]
        xb_blocks.append(xb[CONV_HALO:])
        yield
    xb = jnp.concatenate(xb_blocks, axis=0)

    xb_bf = xb.astype(BF16)
    log_sig_l = _log_sigmoid(w["lru_l"][...])
    y_blocks, h_last = [], h_carry[0:1, :]
    for rows in row_blocks:
        gate_parts = _head_dot(xb_bf[rows], w["w_gate"])
        r_pre = jnp.concatenate([g[:, :MXU_DIM] for g in gate_parts], axis=1)
        i_pre = jnp.concatenate([g[:, MXU_DIM:] for g in gate_parts], axis=1)
        r = jax.nn.sigmoid(r_pre + w["gate_a_b"][...])
        i = jax.nn.sigmoid(i_pre + w["gate_x_b"][...])
        a = jnp.exp(LRU_C * r * log_sig_l)
        one_minus_a2 = 1.0 - a * a
        mult = jnp.where(one_minus_a2 > 0.0, one_minus_a2 * lax.rsqrt(one_minus_a2), 0.0)
        mult = jnp.where(t_idx[rows] == 0, 1.0, mult)
        b = mult * (i * xb[rows])
        yield
        h, h_last = _linear_recurrence(a, b, h_last)
        y_blocks.append((h * jax.nn.gelu(u_gate[rows])).astype(BF16))
        yield
    h_carry[0:1, :] = h_last
    y_lru = jnp.concatenate(y_blocks, axis=0)

    return x_ref[...] + (_dot(y_pool, w["w_out"], rows=slice(0, D_POOL))
                         + _dot(y_lru, w["w_out"], rows=slice(D_POOL, D_POOL + D_LRU)))


def _channel_mixers(h1_ref, p_ref, w):
    z = _rms_norm(h1_ref[...], w["g_mlp"][...]).astype(BF16)
    acts = []
    for c in range(D_FF // DOT_N):
        up = _dot(z, w["w_up"], blocks=[c])
        act = jnp.maximum(up.astype(BF16), 0.0)
        acts.append(act * act)
        yield
    acc = None
    for c, act in enumerate(acts):
        down = _dot(act, w["w_down"], rows=slice(c * DOT_N, (c + 1) * DOT_N))
        acc = down if acc is None else acc + down
        yield
    h2 = h1_ref[...] + acc

    z = _rms_norm(h2, w["g_ple"][...]).astype(BF16)
    ple = _dot(p_ref[...].astype(BF16), w["w_ple_proj"])
    t_rows = z.shape[0]
    outs = []
    for k in range(PLE_ROW_BLOCKS):
        rows = slice(k * t_rows // PLE_ROW_BLOCKS, (k + 1) * t_rows // PLE_ROW_BLOCKS)
        gate = _sigmoid(_dot(z[rows], w["w_ple_gate"]) + w["b_ple_gate"][...])
        h3 = h2[rows] + gate * ple[rows]
        outs.append(_rms_norm(h3, w["g_final"][...]))
    return jnp.concatenate(outs, axis=0)


def _trace_stages(token_stages, channel_stages, order):
    generators = {"T": token_stages, "C": channel_stages}
    results = {}

    def advance(name):
        if name not in results:
            try:
                next(generators[name])
            except StopIteration as stop:
                results[name] = stop.value

    for name in order:
        advance(name)
    for name in generators:
        while name not in results:
            advance(name)
    return results["T"], results["C"]


_VMEM_PARAMS = (
    "g_mix", "w_pool", "pool_b", "pool_scale", "conv_w", "conv_b", "w_gate", "gate_a_b",
    "gate_x_b", "lru_l", "g_mlp", "g_ple", "b_ple_gate", "g_final",
)
_HBM_WEIGHTS = {
    "w_in": (D_MODEL, D_POOL + 2 * D_LRU),
    "w_out": (D_POOL + D_LRU, D_MODEL),
    "w_up": (D_MODEL, D_FF),
    "w_down": (D_FF, D_MODEL),
    "w_ple_gate": (D_MODEL, D_MODEL),
    "w_ple_proj": (PLE_DIM, D_MODEL),
}


def _load_weights(hbm, resident, stage, sems):
    pieces = []
    for name, (k_dim, n_dim) in _HBM_WEIGHTS.items():
        rows = min(k_dim, STAGE_ROWS)
        for j in range(n_dim // DOT_N):
            for r0 in range(0, k_dim, rows):
                pieces.append((hbm[name], r0, j * DOT_N, rows, resident[name], j))

    def copy(i):
        src, r0, c0, rows, _, _ = pieces[i]
        slot = i % STAGE_SLOTS
        return pltpu.make_async_copy(
            src.at[0, pl.ds(r0, rows), pl.ds(c0, DOT_N)],
            stage.at[slot, pl.ds(0, rows), :], sems.at[slot])

    lookahead = STAGE_SLOTS - 1
    for i in range(min(lookahead, len(pieces))):
        copy(i).start()
    for i, (_, r0, _, rows, dst, blk) in enumerate(pieces):
        if i + lookahead < len(pieces):
            copy(i + lookahead).start()
        copy(i).wait()
        slot = i % STAGE_SLOTS

        def cast_rows(k, carry, r0=r0, dst=dst, blk=blk, slot=slot):
            off = pl.multiple_of(k * CAST_ROWS, CAST_ROWS)
            dst[blk, pl.ds(pl.multiple_of(r0 + off, CAST_ROWS), CAST_ROWS), :] = (
                stage[slot, pl.ds(off, CAST_ROWS), :].astype(BF16))
            return carry

        lax.fori_loop(0, rows // CAST_ROWS, cast_rows, 0)


def _block_kernel(tiles_per_seq, x_ref, p_ref, *rest):
    n_vmem, n_hbm = len(_VMEM_PARAMS), len(_HBM_WEIGHTS)
    w = dict(zip(_VMEM_PARAMS, rest[:n_vmem]))
    hbm = dict(zip(_HBM_WEIGHTS, rest[n_vmem:n_vmem + n_hbm]))
    o_ref = rest[n_vmem + n_hbm]
    scratch = rest[n_vmem + n_hbm + 1:]
    resident = dict(zip(_HBM_WEIGHTS, scratch[:n_hbm]))
    stage, sems, pool_halo, conv_halo, h_carry, h1_buf = scratch[n_hbm:]
    w.update(resident)
    step = pl.program_id(0)
    t_rows = x_ref.shape[0]

    tile = step % tiles_per_seq

    def token_mixer():
        return _token_mixer(x_ref, tile == 0, tile * t_rows, w, pool_halo, conv_halo, h_carry)

    @pl.when(step == 0)
    def _():
        _load_weights(hbm, resident, stage, sems)
        h1_new, _ = _trace_stages(token_mixer(), iter(()), "")
        h1_buf[...] = h1_new

    last_step = pl.num_programs(0) - 1

    @pl.when(jnp.logical_and(step > 0, step < last_step))
    def _():
        h1_new, out = _trace_stages(token_mixer(), _channel_mixers(h1_buf, p_ref, w), STAGE_ORDER)
        o_ref[...] = out
        h1_buf[...] = h1_new

    @pl.when(step == last_step)
    def _():
        _, out = _trace_stages(iter(()), _channel_mixers(h1_buf, p_ref, w), "")
        o_ref[...] = out


def _block_diag(w, group):
    heads, n, _ = w.shape
    w = w.reshape(heads // group, group, n, n)
    eye = jnp.eye(group, dtype=w.dtype)
    out = jnp.einsum('bgij,gh->bgihj', w, eye)
    return out.reshape(heads // group, group * n, group * n)


def _resident(shape):
    zeros = (0,) * len(shape)
    return pl.BlockSpec(shape, lambda s: zeros, pipeline_mode=pl.Buffered(1))


@jax.jit
def kernel(x, p, norm_mix_g, w_in, pool_w, pool_b, pool_scale, conv_w, conv_b, gate_a_w, gate_a_b, gate_x_w, gate_x_b, lru_L, w_out, norm_mlp_g, w_up, w_down, norm_ple_g, w_ple_gate, b_ple_gate, w_ple_proj, norm_final_g):
    batch, seq, d_model = x.shape
    assert d_model == D_MODEL and seq % SEQ_TILE == 0
    assert (SEQ_TILE // ROW_BLOCKS) % SUBLANES == 0 and SEQ_TILE // ROW_BLOCKS >= POOL_HALO
    assert p.shape[0] == 1 and w_in.shape[0] == 1, "single-layer block"
    tiles_per_seq = seq // SEQ_TILE
    n_tiles = batch * tiles_per_seq

    row = lambda v: v.reshape(1, -1).astype(F32)
    w_pool = _block_diag(pool_w[0], MXU_DIM // POOL_GROUP_WIDTH).astype(BF16)
    w_gate = jnp.concatenate(
        [_block_diag(gate_a_w[0], MXU_DIM // LRU_BLOCK),
         _block_diag(gate_x_w[0], MXU_DIM // LRU_BLOCK)], axis=-1).astype(BF16)

    vmem_params = dict(
        g_mix=row(norm_mix_g[0]), w_pool=w_pool, pool_b=row(pool_b[0]),
        pool_scale=row(pool_scale[0]), conv_w=conv_w[0].astype(F32), conv_b=row(conv_b[0]),
        w_gate=w_gate, gate_a_b=row(gate_a_b[0]), gate_x_b=row(gate_x_b[0]), lru_l=row(lru_L[0]),
        g_mlp=row(norm_mlp_g[0]), g_ple=row(norm_ple_g[0]), b_ple_gate=row(b_ple_gate[0]),
        g_final=row(norm_final_g),
    )
    vmem_args = [vmem_params[name] for name in _VMEM_PARAMS]
    hbm_weights = dict(w_in=w_in, w_out=w_out, w_up=w_up, w_down=w_down,
                       w_ple_gate=w_ple_gate, w_ple_proj=w_ple_proj)
    hbm_args = [hbm_weights[name] for name in _HBM_WEIGHTS]
    for arg, (name, shape) in zip(hbm_args, _HBM_WEIGHTS.items()):
        assert arg.shape == (1,) + shape and arg.dtype == F32, name

    def tile_spec(width, lag):
        def index_map(s):
            t = jnp.clip(s - lag, 0, n_tiles - 1)
            return (t // tiles_per_seq, t % tiles_per_seq, 0)
        return pl.BlockSpec((None, SEQ_TILE, width), index_map)

    return pl.pallas_call(
        functools.partial(_block_kernel, tiles_per_seq),
        grid=(n_tiles + 1,),
        in_specs=[tile_spec(D_MODEL, 0), tile_spec(PLE_DIM, 1)]
                 + [_resident(a.shape) for a in vmem_args]
                 + [pl.BlockSpec(memory_space=pl.ANY)] * len(hbm_args),
        out_specs=tile_spec(D_MODEL, 1),
        out_shape=jax.ShapeDtypeStruct(x.shape, x.dtype),
        scratch_shapes=[pltpu.VMEM((n // DOT_N, k, DOT_N), BF16) for k, n in _HBM_WEIGHTS.values()]
        + [
            pltpu.VMEM((STAGE_SLOTS, STAGE_ROWS, DOT_N), F32),
            pltpu.SemaphoreType.DMA((STAGE_SLOTS,)),
            pltpu.VMEM((POOL_HALO, D_POOL), F32),
            pltpu.VMEM((CONV_HALO, D_LRU), F32),
            pltpu.VMEM((SUBLANES, D_LRU), F32),
            pltpu.VMEM((SEQ_TILE, D_MODEL), F32),
        ],
        compiler_params=pltpu.CompilerParams(
            dimension_semantics=("arbitrary",),
            vmem_limit_bytes=VMEM_LIMIT_BYTES,
        ),
        name="hybrid_block",
    )(x, p[0], *vmem_args, *hbm_args)
```

```python
import functools

import jax
import jax.numpy as jnp
from jax import lax
from jax.experimental import pallas as pl
from jax.experimental.pallas import tpu as pltpu

D_MODEL = 1024
D_POOL = 512
D_LRU = 512
POOL_WINDOWS = (2, 4, 8, 16)
POOL_GROUP_WIDTH = D_POOL // len(POOL_WINDOWS)
LRU_HEADS = 8
LRU_BLOCK = D_LRU // LRU_HEADS
CONV_WIDTH = 4
LRU_C = 8.0
D_FF = 4 * D_MODEL
PLE_DIM = 256
RMS_EPS = 1e-6

SUBLANES = 8
MXU_DIM = 256
VMEM_LIMIT_BYTES = 56 * 1024 * 1024

SEQ_TILE = 256
DOT_N = 2 * MXU_DIM
ROW_BLOCKS = 2
PLE_ROW_BLOCKS = 2
POOL_HALO = max(POOL_WINDOWS)
CONV_HALO = SUBLANES
STAGE_ROWS = 512
STAGE_SLOTS = 3
CAST_ROWS = 128

_FF_BLOCKS = D_FF // DOT_N
assert _FF_BLOCKS - _FF_BLOCKS // 2 >= ROW_BLOCKS
STAGE_ORDER = ("T" + "C" * (_FF_BLOCKS // 2) + "T" * (2 * ROW_BLOCKS + 1)
               + "TTC" * ROW_BLOCKS + "C" * (_FF_BLOCKS - _FF_BLOCKS // 2 - ROW_BLOCKS)
               + "C" * _FF_BLOCKS + "T" + "C")

F32 = jnp.float32
BF16 = jnp.bfloat16


def _rms_norm(x, g):
    ms = jnp.mean(x * x, axis=-1, keepdims=True)
    return x * lax.rsqrt(ms + RMS_EPS) * g


def _dot(a, w_ref, blocks=None, rows=slice(None)):
    blocks = range(w_ref.shape[0]) if blocks is None else blocks
    outs = [jnp.dot(a, w_ref[j, rows, :], preferred_element_type=F32) for j in blocks]
    return outs[0] if len(outs) == 1 else jnp.concatenate(outs, axis=1)


def _head_dot(a, w_ref):
    return [jnp.dot(a[:, h * MXU_DIM:(h + 1) * MXU_DIM], w_ref[h], preferred_element_type=F32)
            for h in range(w_ref.shape[0])]


def _sigmoid(x):
    return 0.5 * jnp.tanh(0.5 * x) + 0.5


def _log_sigmoid(x):
    return -(jnp.maximum(-x, 0.0) + jnp.log1p(jnp.exp(-jnp.abs(x))))


def _linear_recurrence(a, b, h_prev):
    t_rows, width = a.shape
    sub = lax.broadcasted_iota(jnp.int32, (SUBLANES, width), 0)
    groups = []
    for k in range(t_rows // SUBLANES):
        av = a[k * SUBLANES:(k + 1) * SUBLANES]
        bv = b[k * SUBLANES:(k + 1) * SUBLANES]
        shift = 1
        while shift < SUBLANES:
            keep = sub >= shift
            bv = jnp.where(keep, av * pltpu.roll(bv, shift, axis=0) + bv, bv)
            av = jnp.where(keep, av * pltpu.roll(av, shift, axis=0), av)
            shift *= 2
        h = av * h_prev + bv
        groups.append(h)
        h_prev = h[SUBLANES - 1:SUBLANES, :]
    return jnp.concatenate(groups, axis=0), h_prev


def _token_mixer(x_ref, first_tile, row0, w, pool_halo, conv_halo, h_carry):
    t_rows = x_ref.shape[0]
    row_blocks = [slice(k * t_rows // ROW_BLOCKS, (k + 1) * t_rows // ROW_BLOCKS)
                  for k in range(ROW_BLOCKS)]

    @pl.when(first_tile)
    def _():
        pool_halo[...] = jnp.zeros_like(pool_halo)
        conv_halo[...] = jnp.zeros_like(conv_halo)
        h_carry[...] = jnp.zeros_like(h_carry)

    z = _rms_norm(x_ref[...], w["g_mix"][...]).astype(BF16)
    u_pool = _dot(z, w["w_in"], blocks=[0])
    u_lru = _dot(z, w["w_in"], blocks=[1])
    u_gate = _dot(z, w["w_in"], blocks=[2])
    yield

    t_idx = row0 + lax.broadcasted_iota(jnp.int32, (t_rows, 1), 0)

    ext = jnp.concatenate([pool_halo[...], u_pool], axis=0)
    pool_halo[...] = u_pool[t_rows - POOL_HALO:, :]
    d_blocks = []
    for rows in row_blocks:
        ext_rows = ext[rows.start:rows.stop + POOL_HALO]
        d_groups = []
        for g, win in enumerate(POOL_WINDOWS):
            ug = ext_rows[:, g * POOL_GROUP_WIDTH:(g + 1) * POOL_GROUP_WIDTH]
            s = ug
            k = 1
            while k < win:
                s = s + pltpu.roll(s, k, axis=0)
                k *= 2
            if rows.start < win:
                count = jnp.minimum(t_idx[rows] + 1, win).astype(F32)
                mean = s[POOL_HALO:] / count
            else:
                mean = s[POOL_HALO:] * (1.0 / win)
            d_groups.append(mean - ug[POOL_HALO:])
        d_blocks.append(jnp.concatenate(d_groups, axis=1).astype(BF16))
        yield
    d = jnp.concatenate(d_blocks, axis=0)
    y_pool = jnp.concatenate(_head_dot(d, w["w_pool"]), axis=1)
    y_pool = ((y_pool + w["pool_b"][...]) * w["pool_scale"][...]).astype(BF16)
    yield

    ext = jnp.concatenate([conv_halo[...], u_lru], axis=0)
    conv_halo[...] = u_lru[t_rows - CONV_HALO:, :]
    conv_w = w["conv_w"][...]
    xb_blocks = []
    for rows in row_blocks:
        ext_rows = ext[rows.start:rows.stop + CONV_HALO]
        xb = w["conv_b"][...]
        for k in range(CONV_WIDTH):
            lag = CONV_WIDTH - 1 - k
            tap = ext_rows if lag == 0 else pltpu.roll(ext_rows, lag, axis=0)
            xb = xb + tap * conv_w[k:k + 1, :]
        xb_blocks.append(xb[CONV_HALO:])
        yield
    xb = jnp.concatenate(xb_blocks, axis=0)

    xb_bf = xb.astype(BF16)
    log_sig_l = _log_sigmoid(w["lru_l"][...])
    y_blocks, h_last = [], h_carry[0:1, :]
    for rows in row_blocks:
        gate_parts = _head_dot(xb_bf[rows], w["w_gate"])
        r_pre = jnp.concatenate([g[:, :MXU_DIM] for g in gate_parts], axis=1)
        i_pre = jnp.concatenate([g[:, MXU_DIM:] for g in gate_parts], axis=1)
        r = jax.nn.sigmoid(r_pre + w["gate_a_b"][...])
        i = jax.nn.sigmoid(i_pre + w["gate_x_b"][...])
        a = jnp.exp(LRU_C * r * log_sig_l)
        one_minus_a2 = 1.0 - a * a
        mult = jnp.where(one_minus_a2 > 0.0, one_minus_a2 * lax.rsqrt(one_minus_a2), 0.0)
        mult = jnp.where(t_idx[rows] == 0, 1.0, mult)
        b = mult * (i * xb[rows])
        yield
        h, h_last = _linear_recurrence(a, b, h_last)
        y_blocks.append((h * jax.nn.gelu(u_gate[rows])).astype(BF16))
        yield
    h_carry[0:1, :] = h_last
    y_lru = jnp.concatenate(y_blocks, axis=0)

    return x_ref[...] + (_dot(y_pool, w["w_out"], rows=slice(0, D_POOL))
                         + _dot(y_lru, w["w_out"], rows=slice(D_POOL, D_POOL + D_LRU)))


def _channel_mixers(h1_ref, p_ref, w):
    z = _rms_norm(h1_ref[...], w["g_mlp"][...]).astype(BF16)
    acts = []
    for c in range(D_FF // DOT_N):
        up = _dot(z, w["w_up"], blocks=[c])
        act = jnp.maximum(up.astype(BF16), 0.0)
        acts.append(act * act)
        yield
    acc = None
    for c, act in enumerate(acts):
        down = _dot(act, w["w_down"], rows=slice(c * DOT_N, (c + 1) * DOT_N))
        acc = down if acc is None else acc + down
        yield
    h2 = h1_ref[...] + acc

    z = _rms_norm(h2, w["g_ple"][...]).astype(BF16)
    ple = _dot(p_ref[...].astype(BF16), w["w_ple_proj"])
    t_rows = z.shape[0]
    outs = []
    for k in range(PLE_ROW_BLOCKS):
        rows = slice(k * t_rows // PLE_ROW_BLOCKS, (k + 1) * t_rows // PLE_ROW_BLOCKS)
        gate = _sigmoid(_dot(z[rows], w["w_ple_gate"]) + w["b_ple_gate"][...])
        h3 = h2[rows] + gate * ple[rows]
        outs.append(_rms_norm(h3, w["g_final"][...]))
    return jnp.concatenate(outs, axis=0)


def _trace_stages(token_stages, channel_stages, order):
    generators = {"T": token_stages, "C": channel_stages}
    results = {}

    def advance(name):
        if name not in results:
            try:
                next(generators[name])
            except StopIteration as stop:
                results[name] = stop.value

    for name in order:
        advance(name)
    for name in generators:
        while name not in results:
            advance(name)
    return results["T"], results["C"]


_VMEM_PARAMS = (
    "g_mix", "w_pool", "pool_b", "pool_scale", "conv_w", "conv_b", "w_gate", "gate_a_b",
    "gate_x_b", "lru_l", "g_mlp", "g_ple", "b_ple_gate", "g_final",
)
_HBM_WEIGHTS = {
    "w_in": (D_MODEL, D_POOL + 2 * D_LRU),
    "w_out": (D_POOL + D_LRU, D_MODEL),
    "w_up": (D_MODEL, D_FF),
    "w_down": (D_FF, D_MODEL),
    "w_ple_gate": (D_MODEL, D_MODEL),
    "w_ple_proj": (PLE_DIM, D_MODEL),
}


def _load_weights(hbm, resident, stage, sems):
    pieces = []
    for name, (k_dim, n_dim) in _HBM_WEIGHTS.items():
        rows = min(k_dim, STAGE_ROWS)
        for j in range(n_dim // DOT_N):
            for r0 in range(0, k_dim, rows):
                pieces.append((hbm[name], r0, j * DOT_N, rows, resident[name], j))

    def copy(i):
        src, r0, c0, rows, _, _ = pieces[i]
        slot = i % STAGE_SLOTS
        return pltpu.make_async_copy(
            src.at[0, pl.ds(r0, rows), pl.ds(c0, DOT_N)],
            stage.at[slot, pl.ds(0, rows), :], sems.at[slot])

    lookahead = STAGE_SLOTS - 1
    for i in range(min(lookahead, len(pieces))):
        copy(i).start()
    for i, (_, r0, _, rows, dst, blk) in enumerate(pieces):
        if i + lookahead < len(pieces):
            copy(i + lookahead).start()
        copy(i).wait()
        slot = i % STAGE_SLOTS

        def cast_rows(k, carry, r0=r0, dst=dst, blk=blk, slot=slot):
            off = pl.multiple_of(k * CAST_ROWS, CAST_ROWS)
            dst[blk, pl.ds(pl.multiple_of(r0 + off, CAST_ROWS), CAST_ROWS), :] = (
                stage[slot, pl.ds(off, CAST_ROWS), :].astype(BF16))
            return carry

        lax.fori_loop(0, rows // CAST_ROWS, cast_rows, 0)


def _block_kernel(tiles_per_seq, x_ref, p_ref, *rest):
    n_vmem, n_hbm = len(_VMEM_PARAMS), len(_HBM_WEIGHTS)
    w = dict(zip(_VMEM_PARAMS, rest[:n_vmem]))
    hbm = dict(zip(_HBM_WEIGHTS, rest[n_vmem:n_vmem + n_hbm]))
    o_ref = rest[n_vmem + n_hbm]
    scratch = rest[n_vmem + n_hbm + 1:]
    resident = dict(zip(_HBM_WEIGHTS, scratch[:n_hbm]))
    stage, sems, pool_halo, conv_halo, h_carry, h1_buf = scratch[n_hbm:]
    w.update(resident)
    step = pl.program_id(0)
    t_rows = x_ref.shape[0]

    tile = step % tiles_per_seq

    def token_mixer():
        return _token_mixer(x_ref, tile == 0, tile * t_rows, w, pool_halo, conv_halo, h_carry)

    @pl.when(step == 0)
    def _():
        _load_weights(hbm, resident, stage, sems)
        h1_new, _ = _trace_stages(token_mixer(), iter(()), "")
        h1_buf[...] = h1_new

    last_step = pl.num_programs(0) - 1

    @pl.when(jnp.logical_and(step > 0, step < last_step))
    def _():
        h1_new, out = _trace_stages(token_mixer(), _channel_mixers(h1_buf, p_ref, w), STAGE_ORDER)
        o_ref[...] = out
        h1_buf[...] = h1_new

    @pl.when(step == last_step)
    def _():
        _, out = _trace_stages(iter(()), _channel_mixers(h1_buf, p_ref, w), "")
        o_ref[...] = out


def _block_diag(w, group):
    heads, n, _ = w.shape
    w = w.reshape(heads // group, group, n, n)
    eye = jnp.eye(group, dtype=w.dtype)
    out = jnp.einsum('bgij,gh->bgihj', w, eye)
    return out.reshape(heads // group, group * n, group * n)


def _resident(shape):
    zeros = (0,) * len(shape)
    return pl.BlockSpec(shape, lambda s: zeros, pipeline_mode=pl.Buffered(1))


@jax.jit
def kernel(x, p, norm_mix_g, w_in, pool_w, pool_b, pool_scale, conv_w, conv_b, gate_a_w, gate_a_b, gate_x_w, gate_x_b, lru_L, w_out, norm_mlp_g, w_up, w_down, norm_ple_g, w_ple_gate, b_ple_gate, w_ple_proj, norm_final_g):
    batch, seq, d_model = x.shape
    assert d_model == D_MODEL and seq % SEQ_TILE == 0
    assert (SEQ_TILE // ROW_BLOCKS) % SUBLANES == 0 and SEQ_TILE // ROW_BLOCKS >= POOL_HALO
    assert p.shape[0] == 1 and w_in.shape[0] == 1, "single-layer block"
    tiles_per_seq = seq // SEQ_TILE
    n_tiles = batch * tiles_per_seq

    row = lambda v: v.reshape(1, -1).astype(F32)
    w_pool = _block_diag(pool_w[0], MXU_DIM // POOL_GROUP_WIDTH).astype(BF16)
    w_gate = jnp.concatenate(
        [_block_diag(gate_a_w[0], MXU_DIM // LRU_BLOCK),
         _block_diag(gate_x_w[0], MXU_DIM // LRU_BLOCK)], axis=-1).astype(BF16)

    vmem_params = dict(
        g_mix=row(norm_mix_g[0]), w_pool=w_pool, pool_b=row(pool_b[0]),
        pool_scale=row(pool_scale[0]), conv_w=conv_w[0].astype(F32), conv_b=row(conv_b[0]),
        w_gate=w_gate, gate_a_b=row(gate_a_b[0]), gate_x_b=row(gate_x_b[0]), lru_l=row(lru_L[0]),
        g_mlp=row(norm_mlp_g[0]), g_ple=row(norm_ple_g[0]), b_ple_gate=row(b_ple_gate[0]),
        g_final=row(norm_final_g),
    )
    vmem_args = [vmem_params[name] for name in _VMEM_PARAMS]
    hbm_weights = dict(w_in=w_in, w_out=w_out, w_up=w_up, w_down=w_down,
                       w_ple_gate=w_ple_gate, w_ple_proj=w_ple_proj)
    hbm_args = [hbm_weights[name] for name in _HBM_WEIGHTS]
    for arg, (name, shape) in zip(hbm_args, _HBM_WEIGHTS.items()):
        assert arg.shape == (1,) + shape and arg.dtype == F32, name

    def tile_spec(width, lag):
        def index_map(s):
            t = jnp.clip(s - lag, 0, n_tiles - 1)
            return (t // tiles_per_seq, t % tiles_per_seq, 0)
        return pl.BlockSpec((None, SEQ_TILE, width), index_map)

    return pl.pallas_call(
        functools.partial(_block_kernel, tiles_per_seq),
        grid=(n_tiles + 1,),
        in_specs=[tile_spec(D_MODEL, 0), tile_spec(PLE_DIM, 1)]
                 + [_resident(a.shape) for a in vmem_args]
                 + [pl.BlockSpec(memory_space=pl.ANY)] * len(hbm_args),
        out_specs=tile_spec(D_MODEL, 1),
        out_shape=jax.ShapeDtypeStruct(x.shape, x.dtype),
        scratch_shapes=[pltpu.VMEM((n // DOT_N, k, DOT_N), BF16) for k, n in _HBM_WEIGHTS.values()]
        + [
            pltpu.VMEM((STAGE_SLOTS, STAGE_ROWS, DOT_N), F32),
            pltpu.SemaphoreType.DMA((STAGE_SLOTS,)),
            pltpu.VMEM((POOL_HALO, D_POOL), F32),
            pltpu.VMEM((CONV_HALO, D_LRU), F32),
            pltpu.VMEM((SUBLANES, D_LRU), F32),
            pltpu.VMEM((SEQ_TILE, D_MODEL), F32),
        ],
        compiler_params=pltpu.CompilerParams(
            dimension_semantics=("arbitrary",),
            vmem_limit_bytes=VMEM_LIMIT_BYTES,
        ),
        name="hybrid_block",
    )(x, p[0], *vmem_args, *hbm_args)
```

```python
import functools

import jax
import jax.numpy as jnp
from jax import lax
from jax.experimental import pallas as pl
from jax.experimental.pallas import tpu as pltpu

D_MODEL = 1024
D_POOL = 512
D_LRU = 512
POOL_WINDOWS = (2, 4, 8, 16)
POOL_GROUP_WIDTH = D_POOL // len(POOL_WINDOWS)
LRU_HEADS = 8
LRU_BLOCK = D_LRU // LRU_HEADS
CONV_WIDTH = 4
LRU_C = 8.0
D_FF = 4 * D_MODEL
PLE_DIM = 256
RMS_EPS = 1e-6

SUBLANES = 8
MXU_DIM = 256
VMEM_LIMIT_BYTES = 56 * 1024 * 1024

SEQ_TILE = 512
DOT_N = 2 * MXU_DIM
ROW_BLOCKS = 4
PLE_ROW_BLOCKS = 2
POOL_HALO = max(POOL_WINDOWS)
CONV_HALO = SUBLANES
STAGE_ROWS = 512
STAGE_SLOTS = 3
CAST_ROWS = 128

_FF_BLOCKS = D_FF // DOT_N
assert _FF_BLOCKS - _FF_BLOCKS // 2 >= ROW_BLOCKS
STAGE_ORDER = ("T" + "C" * (_FF_BLOCKS // 2) + "T" * (2 * ROW_BLOCKS + 1)
               + "TTC" * ROW_BLOCKS + "C" * (_FF_BLOCKS - _FF_BLOCKS // 2 - ROW_BLOCKS)
               + "C" * _FF_BLOCKS + "T" + "C")

F32 = jnp.float32
BF16 = jnp.bfloat16


def _rms_norm(x, g):
    ms = jnp.mean(x * x, axis=-1, keepdims=True)
    return x * lax.rsqrt(ms + RMS_EPS) * g


def _dot(a, w_ref, blocks=None, rows=slice(None)):
    blocks = range(w_ref.shape[0]) if blocks is None else blocks
    outs = [jnp.dot(a, w_ref[j, rows, :], preferred_element_type=F32) for j in blocks]
    return outs[0] if len(outs) == 1 else jnp.concatenate(outs, axis=1)


def _head_dot(a, w_ref):
    return [jnp.dot(a[:, h * MXU_DIM:(h + 1) * MXU_DIM], w_ref[h], preferred_element_type=F32)
            for h in range(w_ref.shape[0])]


def _sigmoid(x):
    return 0.5 * jnp.tanh(0.5 * x) + 0.5


def _log_sigmoid(x):
    return -(jnp.maximum(-x, 0.0) + jnp.log1p(jnp.exp(-jnp.abs(x))))


def _linear_recurrence(a, b, h_prev):
    t_rows, width = a.shape
    sub = lax.broadcasted_iota(jnp.int32, (SUBLANES, width), 0)
    groups = []
    for k in range(t_rows // SUBLANES):
        av = a[k * SUBLANES:(k + 1) * SUBLANES]
        bv = b[k * SUBLANES:(k + 1) * SUBLANES]
        shift = 1
        while shift < SUBLANES:
            keep = sub >= shift
            bv = jnp.where(keep, av * pltpu.roll(bv, shift, axis=0) + bv, bv)
            av = jnp.where(keep, av * pltpu.roll(av, shift, axis=0), av)
            shift *= 2
        h = av * h_prev + bv
        groups.append(h)
        h_prev = h[SUBLANES - 1:SUBLANES, :]
    return jnp.concatenate(groups, axis=0), h_prev


def _token_mixer(x_ref, first_tile, row0, w, pool_halo, conv_halo, h_carry):
    t_rows = x_ref.shape[0]
    row_blocks = [slice(k * t_rows // ROW_BLOCKS, (k + 1) * t_rows // ROW_BLOCKS)
                  for k in range(ROW_BLOCKS)]

    @pl.when(first_tile)
    def _():
        pool_halo[...] = jnp.zeros_like(pool_halo)
        conv_halo[...] = jnp.zeros_like(conv_halo)
        h_carry[...] = jnp.zeros_like(h_carry)

    z = _rms_norm(x_ref[...], w["g_mix"][...]).astype(BF16)
    u_pool = _dot(z, w["w_in"], blocks=[0])
    u_lru = _dot(z, w["w_in"], blocks=[1])
    u_gate = _dot(z, w["w_in"], blocks=[2])
    yield

    t_idx = row0 + lax.broadcasted_iota(jnp.int32, (t_rows, 1), 0)

    ext = jnp.concatenate([pool_halo[...], u_pool], axis=0)
    pool_halo[...] = u_pool[t_rows - POOL_HALO:, :]
    d_blocks = []
    for rows in row_blocks:
        ext_rows = ext[rows.start:rows.stop + POOL_HALO]
        d_groups = []
        for g, win in enumerate(POOL_WINDOWS):
            ug = ext_rows[:, g * POOL_GROUP_WIDTH:(g + 1) * POOL_GROUP_WIDTH]
            s = ug
            k = 1
            while k < win:
                s = s + pltpu.roll(s, k, axis=0)
                k *= 2
            if rows.start < win:
                count = jnp.minimum(t_idx[rows] + 1, win).astype(F32)
                mean = s[POOL_HALO:] / count
            else:
                mean = s[POOL_HALO:] * (1.0 / win)
            d_groups.append(mean - ug[POOL_HALO:])
        d_blocks.append(jnp.concatenate(d_groups, axis=1).astype(BF16))
        yield
    d = jnp.concatenate(d_blocks, axis=0)
    y_pool = jnp.concatenate(_head_dot(d, w["w_pool"]), axis=1)
    y_pool = ((y_pool + w["pool_b"][...]) * w["pool_scale"][...]).astype(BF16)
    yield

    ext = jnp.concatenate([conv_halo[...], u_lru], axis=0)
    conv_halo[...] = u_lru[t_rows - CONV_HALO:, :]
    conv_w = w["conv_w"][...]
    xb_blocks = []
    for rows in row_blocks:
        ext_rows = ext[rows.start:rows.stop + CONV_HALO]
        xb = w["conv_b"][...]
        for k in range(CONV_WIDTH):
            lag = CONV_WIDTH - 1 - k
            tap = ext_rows if lag == 0 else pltpu.roll(ext_rows, lag, axis=0)
            xb = xb + tap * conv_w[k:k + 1, :]
        xb_blocks.append(xb[CONV_HALO:])
        yield
    xb = jnp.concatenate(xb_blocks, axis=0)

    xb_bf = xb.astype(BF16)
    log_sig_l = _log_sigmoid(w["lru_l"][...])
    y_blocks, h_last = [], h_carry[0:1, :]
    for rows in row_blocks:
        gate_parts = _head_dot(xb_bf[rows], w["w_gate"])
        r_pre = jnp.concatenate([g[:, :MXU_DIM] for g in gate_parts], axis=1)
        i_pre = jnp.concatenate([g[:, MXU_DIM:] for g in gate_parts], axis=1)
        r = jax.nn.sigmoid(r_pre + w["gate_a_b"][...])
        i = jax.nn.sigmoid(i_pre + w["gate_x_b"][...])
        a = jnp.exp(LRU_C * r * log_sig_l)
        one_minus_a2 = 1.0 - a * a
        mult = jnp.where(one_minus_a2 > 0.0, one_minus_a2 * lax.rsqrt(one_minus_a2), 0.0)
        mult = jnp.where(t_idx[rows] == 0, 1.0, mult)
        b = mult * (i * xb[rows])
        yield
        h, h_last = _linear_recurrence(a, b, h_last)
        y_blocks.append((h * jax.nn.gelu(u_gate[rows])).astype(BF16))
        yield
    h_carry[0:1, :] = h_last
    y_lru = jnp.concatenate(y_blocks, axis=0)

    return x_ref[...] + (_dot(y_pool, w["w_out"], rows=slice(0, D_POOL))
                         + _dot(y_lru, w["w_out"], rows=slice(D_POOL, D_POOL + D_LRU)))


def _channel_mixers(h1_ref, p_ref, w):
    z = _rms_norm(h1_ref[...], w["g_mlp"][...]).astype(BF16)
    acts = []
    for c in range(D_FF // DOT_N):
        up = _dot(z, w["w_up"], blocks=[c])
        act = jnp.maximum(up.astype(BF16), 0.0)
        acts.append(act * act)
        yield
    acc = None
    for c, act in enumerate(acts):
        down = _dot(act, w["w_down"], rows=slice(c * DOT_N, (c + 1) * DOT_N))
        acc = down if acc is None else acc + down
        yield
    h2 = h1_ref[...] + acc

    z = _rms_norm(h2, w["g_ple"][...]).astype(BF16)
    ple = _dot(p_ref[...].astype(BF16), w["w_ple_proj"])
    t_rows = z.shape[0]
    outs = []
    for k in range(PLE_ROW_BLOCKS):
        rows = slice(k * t_rows // PLE_ROW_BLOCKS, (k + 1) * t_rows // PLE_ROW_BLOCKS)
        gate = _sigmoid(_dot(z[rows], w["w_ple_gate"]) + w["b_ple_gate"][...])
        h3 = h2[rows] + gate * ple[rows]
        outs.append(_rms_norm(h3, w["g_final"][...]))
    return jnp.concatenate(outs, axis=0)


def _trace_stages(token_stages, channel_stages, order):
    generators = {"T": token_stages, "C": channel_stages}
    results = {}

    def advance(name):
        if name not in results:
            try:
                next(generators[name])
            except StopIteration as stop:
                results[name] = stop.value

    for name in order:
        advance(name)
    for name in generators:
        while name not in results:
            advance(name)
    return results["T"], results["C"]


_VMEM_PARAMS = (
    "g_mix", "pool_w", "pool_b", "pool_scale", "conv_w", "conv_b", "gate_a_w", "gate_x_w",
    "gate_a_b_heads", "gate_x_b_heads", "lru_l", "g_mlp", "g_ple", "b_ple_gate", "g_final",
)
_HBM_WEIGHTS = {
    "w_in": (D_MODEL, D_POOL + 2 * D_LRU),
    "w_out": (D_POOL + D_LRU, D_MODEL),
    "w_up": (D_MODEL, D_FF),
    "w_down": (D_FF, D_MODEL),
    "w_ple_gate": (D_MODEL, D_MODEL),
    "w_ple_proj": (PLE_DIM, D_MODEL),
}


def _load_weights(hbm, resident, stage, sems):
    pieces = []
    for name, (k_dim, n_dim) in _HBM_WEIGHTS.items():
        rows = min(k_dim, STAGE_ROWS)
        for j in range(n_dim // DOT_N):
            for r0 in range(0, k_dim, rows):
                pieces.append((hbm[name], r0, j * DOT_N, rows, resident[name], j))

    def copy(i):
        src, r0, c0, rows, _, _ = pieces[i]
        slot = i % STAGE_SLOTS
        return pltpu.make_async_copy(
            src.at[0, pl.ds(r0, rows), pl.ds(c0, DOT_N)],
            stage.at[slot, pl.ds(0, rows), :], sems.at[slot])

    lookahead = STAGE_SLOTS - 1
    for i in range(min(lookahead, len(pieces))):
        copy(i).start()
    for i, (_, r0, _, rows, dst, blk) in enumerate(pieces):
        if i + lookahead < len(pieces):
            copy(i + lookahead).start()
        copy(i).wait()
        slot = i % STAGE_SLOTS

        def cast_rows(k, carry, r0=r0, dst=dst, blk=blk, slot=slot):
            off = pl.multiple_of(k * CAST_ROWS, CAST_ROWS)
            dst[blk, pl.ds(pl.multiple_of(r0 + off, CAST_ROWS), CAST_ROWS), :] = (
                stage[slot, pl.ds(off, CAST_ROWS), :].astype(BF16))
            return carry

        lax.fori_loop(0, rows // CAST_ROWS, cast_rows, 0)


def _place_lanes(x, offset, width):
    return jnp.pad(x, ((0, 0), (offset, width - offset - x.shape[1])))


def _assemble_head_params(w, w_pool, w_gate, gate_bias):
    def block_diag(heads_ref, first, count):
        n = heads_ref.shape[-1]
        return jnp.concatenate(
            [_place_lanes(heads_ref[first + g], g * n, count * n) for g in range(count)], axis=0)

    per_block = MXU_DIM // POOL_GROUP_WIDTH
    for blk in range(w_pool.shape[0]):
        w_pool[blk] = block_diag(w["pool_w"], blk * per_block, per_block).astype(BF16)
    per_block = MXU_DIM // LRU_BLOCK
    for blk in range(w_gate.shape[0]):
        w_gate[blk] = jnp.concatenate(
            [block_diag(w["gate_a_w"], blk * per_block, per_block),
             block_diag(w["gate_x_w"], blk * per_block, per_block)], axis=1).astype(BF16)
    for row, name in enumerate(("gate_a_b_heads", "gate_x_b_heads")):
        heads = w[name]
        pieces = [_place_lanes(heads[h:h + 1, :], h * LRU_BLOCK, D_LRU) for h in range(LRU_HEADS)]
        gate_bias[row:row + 1, :] = functools.reduce(jnp.add, pieces)


def _block_kernel(tiles_per_seq, x_ref, p_ref, *rest):
    n_vmem, n_hbm = len(_VMEM_PARAMS), len(_HBM_WEIGHTS)
    w = dict(zip(_VMEM_PARAMS, rest[:n_vmem]))
    hbm = dict(zip(_HBM_WEIGHTS, rest[n_vmem:n_vmem + n_hbm]))
    o_ref = rest[n_vmem + n_hbm]
    scratch = rest[n_vmem + n_hbm + 1:]
    resident = dict(zip(_HBM_WEIGHTS, scratch[:n_hbm]))
    (stage, sems, pool_halo, conv_halo, h_carry, h1_buf,
     w_pool, w_gate, gate_bias) = scratch[n_hbm:]
    w.update(resident)
    w.update(w_pool=w_pool, w_gate=w_gate,
             gate_a_b=gate_bias.at[0:1, :], gate_x_b=gate_bias.at[1:2, :])
    step = pl.program_id(0)
    t_rows = x_ref.shape[0]

    tile = step % tiles_per_seq

    def token_mixer():
        return _token_mixer(x_ref, tile == 0, tile * t_rows, w, pool_halo, conv_halo, h_carry)

    @pl.when(step == 0)
    def _():
        _assemble_head_params(w, w_pool, w_gate, gate_bias)
        _load_weights(hbm, resident, stage, sems)
        h1_new, _ = _trace_stages(token_mixer(), iter(()), "")
        h1_buf[...] = h1_new

    last_step = pl.num_programs(0) - 1

    @pl.when(jnp.logical_and(step > 0, step < last_step))
    def _():
        h1_new, out = _trace_stages(token_mixer(), _channel_mixers(h1_buf, p_ref, w), STAGE_ORDER)
        o_ref[...] = out
        h1_buf[...] = h1_new

    @pl.when(step == last_step)
    def _():
        _, out = _trace_stages(iter(()), _channel_mixers(h1_buf, p_ref, w), "")
        o_ref[...] = out


def _resident(shape):
    zeros = (0,) * len(shape)
    return pl.BlockSpec(shape, lambda s: zeros, pipeline_mode=pl.Buffered(1))


@jax.jit
def kernel(x, p, norm_mix_g, w_in, pool_w, pool_b, pool_scale, conv_w, conv_b, gate_a_w, gate_a_b, gate_x_w, gate_x_b, lru_L, w_out, norm_mlp_g, w_up, w_down, norm_ple_g, w_ple_gate, b_ple_gate, w_ple_proj, norm_final_g):
    batch, seq, d_model = x.shape
    assert d_model == D_MODEL and seq % SEQ_TILE == 0
    assert (SEQ_TILE // ROW_BLOCKS) % SUBLANES == 0 and SEQ_TILE // ROW_BLOCKS >= POOL_HALO
    assert p.shape[0] == 1 and w_in.shape[0] == 1, "single-layer block"
    tiles_per_seq = seq // SEQ_TILE
    n_tiles = batch * tiles_per_seq

    row = lambda v: v.reshape(1, -1).astype(F32)
    vmem_params = dict(
        g_mix=row(norm_mix_g[0]), pool_w=pool_w[0].astype(F32), pool_b=row(pool_b[0]),
        pool_scale=row(pool_scale[0]), conv_w=conv_w[0].astype(F32), conv_b=row(conv_b[0]),
        gate_a_w=gate_a_w[0].astype(F32), gate_x_w=gate_x_w[0].astype(F32),
        gate_a_b_heads=gate_a_b[0].astype(F32), gate_x_b_heads=gate_x_b[0].astype(F32),
        lru_l=row(lru_L[0]), g_mlp=row(norm_mlp_g[0]), g_ple=row(norm_ple_g[0]),
        b_ple_gate=row(b_ple_gate[0]), g_final=row(norm_final_g),
    )
    assert vmem_params["pool_w"].shape == (len(POOL_WINDOWS), POOL_GROUP_WIDTH, POOL_GROUP_WIDTH)
    assert vmem_params["gate_a_w"].shape == (LRU_HEADS, LRU_BLOCK, LRU_BLOCK)
    vmem_args = [vmem_params[name] for name in _VMEM_PARAMS]
    hbm_weights = dict(w_in=w_in, w_out=w_out, w_up=w_up, w_down=w_down,
                       w_ple_gate=w_ple_gate, w_ple_proj=w_ple_proj)
    hbm_args = [hbm_weights[name] for name in _HBM_WEIGHTS]
    for arg, (name, shape) in zip(hbm_args, _HBM_WEIGHTS.items()):
        assert arg.shape == (1,) + shape and arg.dtype == F32, name

    def tile_spec(width, lag):
        def index_map(s):
            t = jnp.clip(s - lag, 0, n_tiles - 1)
            return (t // tiles_per_seq, t % tiles_per_seq, 0)
        return pl.BlockSpec((None, SEQ_TILE, width), index_map)

    return pl.pallas_call(
        functools.partial(_block_kernel, tiles_per_seq),
        grid=(n_tiles + 1,),
        in_specs=[tile_spec(D_MODEL, 0), tile_spec(PLE_DIM, 1)]
                 + [_resident(a.shape) for a in vmem_args]
                 + [pl.BlockSpec(memory_space=pl.ANY)] * len(hbm_args),
        out_specs=tile_spec(D_MODEL, 1),
        out_shape=jax.ShapeDtypeStruct(x.shape, x.dtype),
        scratch_shapes=[pltpu.VMEM((n // DOT_N, k, DOT_N), BF16) for k, n in _HBM_WEIGHTS.values()]
        + [
            pltpu.VMEM((STAGE_SLOTS, STAGE_ROWS, DOT_N), F32),
            pltpu.SemaphoreType.DMA((STAGE_SLOTS,)),
            pltpu.VMEM((POOL_HALO, D_POOL), F32),
            pltpu.VMEM((CONV_HALO, D_LRU), F32),
            pltpu.VMEM((SUBLANES, D_LRU), F32),
            pltpu.VMEM((SEQ_TILE, D_MODEL), F32),
            pltpu.VMEM((D_POOL // MXU_DIM, MXU_DIM, MXU_DIM), BF16),
            pltpu.VMEM((D_LRU // MXU_DIM, MXU_DIM, 2 * MXU_DIM), BF16),
            pltpu.VMEM((SUBLANES, D_LRU), F32),
        ],
        compiler_params=pltpu.CompilerParams(
            dimension_semantics=("arbitrary",),
            vmem_limit_bytes=VMEM_LIMIT_BYTES,
        ),
        name="hybrid_block",
    )(x, p[0], *vmem_args, *hbm_args)
```

```python
import functools

import jax
import jax.numpy as jnp
from jax import lax
from jax.experimental import pallas as pl
from jax.experimental.pallas import tpu as pltpu

D_MODEL = 1024
D_POOL = 512
D_LRU = 512
POOL_WINDOWS = (2, 4, 8, 16)
POOL_GROUP_WIDTH = D_POOL // len(POOL_WINDOWS)
LRU_HEADS = 8
LRU_BLOCK = D_LRU // LRU_HEADS
CONV_WIDTH = 4
LRU_C = 8.0
D_FF = 4 * D_MODEL
PLE_DIM = 256
RMS_EPS = 1e-6

SUBLANES = 8
MXU_DIM = 256
VMEM_LIMIT_BYTES = 56 * 1024 * 1024

SEQ_TILE = 512
DOT_N = 2 * MXU_DIM
ROW_BLOCKS = 4
PLE_ROW_BLOCKS = 2
POOL_HALO = max(POOL_WINDOWS)
CONV_HALO = SUBLANES
STAGE_ROWS = 512
STAGE_SLOTS = 3
CAST_ROWS = 128

_FF_BLOCKS = D_FF // DOT_N
assert _FF_BLOCKS - _FF_BLOCKS // 2 >= ROW_BLOCKS
STAGE_ORDER = ("CT" + "C" * (_FF_BLOCKS // 2 - 1) + "T" * (2 * ROW_BLOCKS + 1)
               + "TTC" * ROW_BLOCKS + "C" * (_FF_BLOCKS - _FF_BLOCKS // 2 - ROW_BLOCKS)
               + "C" * _FF_BLOCKS + "T" + "C")

F32 = jnp.float32
BF16 = jnp.bfloat16


def _rms_norm(x, g):
    ms = jnp.mean(x * x, axis=-1, keepdims=True)
    return x * lax.rsqrt(ms + RMS_EPS) * g


def _dot(a, w_ref, blocks=None, rows=slice(None)):
    blocks = range(w_ref.shape[0]) if blocks is None else blocks
    outs = [jnp.dot(a, w_ref[j, rows, :], preferred_element_type=F32) for j in blocks]
    return outs[0] if len(outs) == 1 else jnp.concatenate(outs, axis=1)


def _head_dot(a, w_ref):
    return [jnp.dot(a[:, h * MXU_DIM:(h + 1) * MXU_DIM], w_ref[h], preferred_element_type=F32)
            for h in range(w_ref.shape[0])]


def _sigmoid(x):
    return 0.5 * jnp.tanh(0.5 * x) + 0.5


def _log_sigmoid(x):
    return -(jnp.maximum(-x, 0.0) + jnp.log1p(jnp.exp(-jnp.abs(x))))


def _linear_recurrence(a, b, h_prev):
    t_rows, width = a.shape
    sub = lax.broadcasted_iota(jnp.int32, (SUBLANES, width), 0)
    groups = []
    for k in range(t_rows // SUBLANES):
        av = a[k * SUBLANES:(k + 1) * SUBLANES]
        bv = b[k * SUBLANES:(k + 1) * SUBLANES]
        shift = 1
        while shift < SUBLANES:
            keep = sub >= shift
            bv = jnp.where(keep, av * pltpu.roll(bv, shift, axis=0) + bv, bv)
            av = jnp.where(keep, av * pltpu.roll(av, shift, axis=0), av)
            shift *= 2
        h = av * h_prev + bv
        groups.append(h)
        h_prev = h[SUBLANES - 1:SUBLANES, :]
    return jnp.concatenate(groups, axis=0), h_prev


def _token_mixer(x_ref, row0, w, pool_halo, conv_halo, h_carry):
    t_rows = x_ref.shape[0]
    row_blocks = [slice(k * t_rows // ROW_BLOCKS, (k + 1) * t_rows // ROW_BLOCKS)
                  for k in range(ROW_BLOCKS)]

    z = _rms_norm(x_ref[...], w["g_mix"][...]).astype(BF16)
    u_pool = _dot(z, w["w_in"], blocks=[0])
    u_lru = _dot(z, w["w_in"], blocks=[1])
    u_gate = _dot(z, w["w_in"], blocks=[2])
    yield

    t_idx = row0 + lax.broadcasted_iota(jnp.int32, (t_rows, 1), 0)

    ext = jnp.concatenate([pool_halo[...], u_pool], axis=0)
    pool_halo[...] = u_pool[t_rows - POOL_HALO:, :]
    d_blocks = []
    for rows in row_blocks:
        ext_rows = ext[rows.start:rows.stop + POOL_HALO]
        d_groups = []
        for g, win in enumerate(POOL_WINDOWS):
            ug = ext_rows[:, g * POOL_GROUP_WIDTH:(g + 1) * POOL_GROUP_WIDTH]
            s = ug
            k = 1
            while k < win:
                s = s + pltpu.roll(s, k, axis=0)
                k *= 2
            if rows.start < win:
                count = jnp.minimum(t_idx[rows] + 1, win).astype(F32)
                mean = s[POOL_HALO:] / count
            else:
                mean = s[POOL_HALO:] * (1.0 / win)
            d_groups.append(mean - ug[POOL_HALO:])
        d_blocks.append(jnp.concatenate(d_groups, axis=1).astype(BF16))
        yield
    d = jnp.concatenate(d_blocks, axis=0)
    y_pool = jnp.concatenate(_head_dot(d, w["w_pool"]), axis=1)
    y_pool = ((y_pool + w["pool_b"][...]) * w["pool_scale"][...]).astype(BF16)
    yield

    ext = jnp.concatenate([conv_halo[...], u_lru], axis=0)
    conv_halo[...] = u_lru[t_rows - CONV_HALO:, :]
    conv_w = w["conv_w"][...]
    xb_blocks = []
    for rows in row_blocks:
        ext_rows = ext[rows.start:rows.stop + CONV_HALO]
        xb = w["conv_b"][...]
        for k in range(CONV_WIDTH):
            lag = CONV_WIDTH - 1 - k
            tap = ext_rows if lag == 0 else pltpu.roll(ext_rows, lag, axis=0)
            xb = xb + tap * conv_w[k:k + 1, :]
        xb_blocks.append(xb[CONV_HALO:])
        yield
    xb = jnp.concatenate(xb_blocks, axis=0)

    xb_bf = xb.astype(BF16)
    log_sig_l = _log_sigmoid(w["lru_l"][...])
    y_blocks, h_last = [], h_carry[0:1, :]
    for rows in row_blocks:
        gate_parts = _head_dot(xb_bf[rows], w["w_gate"])
        r_pre = jnp.concatenate([g[:, :MXU_DIM] for g in gate_parts], axis=1)
        i_pre = jnp.concatenate([g[:, MXU_DIM:] for g in gate_parts], axis=1)
        r = jax.nn.sigmoid(r_pre + w["gate_a_b"][...])
        i = jax.nn.sigmoid(i_pre + w["gate_x_b"][...])
        a = jnp.exp(LRU_C * r * log_sig_l)
        one_minus_a2 = 1.0 - a * a
        mult = jnp.where(one_minus_a2 > 0.0, one_minus_a2 * lax.rsqrt(one_minus_a2), 0.0)
        mult = jnp.where(t_idx[rows] == 0, 1.0, mult)
        b = mult * (i * xb[rows])
        yield
        h, h_last = _linear_recurrence(a, b, h_last)
        y_blocks.append((h * jax.nn.gelu(u_gate[rows])).astype(BF16))
        yield
    h_carry[0:1, :] = h_last
    y_lru = jnp.concatenate(y_blocks, axis=0)

    h1 = x_ref[...] + (_dot(y_pool, w["w_out"], rows=slice(0, D_POOL))
                       + _dot(y_lru, w["w_out"], rows=slice(D_POOL, D_POOL + D_LRU)))
    return h1, _rms_norm(h1, w["g_mlp"][...]).astype(BF16)


def _channel_mixers(h1_ref, p_ref, w, z_buf):
    acts = []
    for c in range(D_FF // DOT_N):
        up = _dot(z_buf[...], w["w_up"], blocks=[c])
        act = jnp.maximum(up.astype(BF16), 0.0)
        acts.append(act * act)
        yield
    acc = None
    for c, act in enumerate(acts):
        down = _dot(act, w["w_down"], rows=slice(c * DOT_N, (c + 1) * DOT_N))
        acc = down if acc is None else acc + down
        yield
    h2 = h1_ref[...] + acc

    z = _rms_norm(h2, w["g_ple"][...]).astype(BF16)
    ple = _dot(p_ref[...].astype(BF16), w["w_ple_proj"])
    t_rows = z.shape[0]
    outs = []
    for k in range(PLE_ROW_BLOCKS):
        rows = slice(k * t_rows // PLE_ROW_BLOCKS, (k + 1) * t_rows // PLE_ROW_BLOCKS)
        gate = _sigmoid(_dot(z[rows], w["w_ple_gate"]) + w["b_ple_gate"][...])
        h3 = h2[rows] + gate * ple[rows]
        outs.append(_rms_norm(h3, w["g_final"][...]))
    return jnp.concatenate(outs, axis=0)


def _trace_stages(token_stages, channel_stages, order):
    generators = {"T": token_stages, "C": channel_stages}
    results = {}

    def advance(name):
        if name not in results:
            try:
                next(generators[name])
            except StopIteration as stop:
                results[name] = stop.value

    for name in order:
        advance(name)
    for name in generators:
        while name not in results:
            advance(name)
    return results["T"], results["C"]


_VMEM_PARAMS = (
    "g_mix", "pool_w", "pool_b", "pool_scale", "conv_w", "conv_b", "gate_a_w", "gate_x_w",
    "gate_a_b_heads", "gate_x_b_heads", "lru_l", "g_mlp", "g_ple", "b_ple_gate", "g_final",
)
_HBM_WEIGHTS = {
    "w_in": (D_MODEL, D_POOL + 2 * D_LRU),
    "w_out": (D_POOL + D_LRU, D_MODEL),
    "w_up": (D_MODEL, D_FF),
    "w_down": (D_FF, D_MODEL),
    "w_ple_gate": (D_MODEL, D_MODEL),
    "w_ple_proj": (PLE_DIM, D_MODEL),
}


def _load_weights(hbm, resident, stage, sems):
    pieces = []
    for name, (k_dim, n_dim) in _HBM_WEIGHTS.items():
        rows = min(k_dim, STAGE_ROWS)
        for j in range(n_dim // DOT_N):
            for r0 in range(0, k_dim, rows):
                pieces.append((hbm[name], r0, j * DOT_N, rows, resident[name], j))

    def copy(i):
        src, r0, c0, rows, _, _ = pieces[i]
        slot = i % STAGE_SLOTS
        return pltpu.make_async_copy(
            src.at[0, pl.ds(r0, rows), pl.ds(c0, DOT_N)],
            stage.at[slot, pl.ds(0, rows), :], sems.at[slot])

    lookahead = STAGE_SLOTS - 1
    for i in range(min(lookahead, len(pieces))):
        copy(i).start()
    for i, (_, r0, _, rows, dst, blk) in enumerate(pieces):
        if i + lookahead < len(pieces):
            copy(i + lookahead).start()
        copy(i).wait()
        slot = i % STAGE_SLOTS

        def cast_rows(k, carry, r0=r0, dst=dst, blk=blk, slot=slot):
            off = pl.multiple_of(k * CAST_ROWS, CAST_ROWS)
            dst[blk, pl.ds(pl.multiple_of(r0 + off, CAST_ROWS), CAST_ROWS), :] = (
                stage[slot, pl.ds(off, CAST_ROWS), :].astype(BF16))
            return carry

        lax.fori_loop(0, rows // CAST_ROWS, cast_rows, 0)


def _place_lanes(x, offset, width):
    return jnp.pad(x, ((0, 0), (offset, width - offset - x.shape[1])))


def _assemble_head_params(w, w_pool, w_gate, gate_bias):
    def block_diag(heads_ref, first, count):
        n = heads_ref.shape[-1]
        return jnp.concatenate(
            [_place_lanes(heads_ref[first + g], g * n, count * n) for g in range(count)], axis=0)

    per_block = MXU_DIM // POOL_GROUP_WIDTH
    for blk in range(w_pool.shape[0]):
        w_pool[blk] = block_diag(w["pool_w"], blk * per_block, per_block).astype(BF16)
    per_block = MXU_DIM // LRU_BLOCK
    for blk in range(w_gate.shape[0]):
        w_gate[blk] = jnp.concatenate(
            [block_diag(w["gate_a_w"], blk * per_block, per_block),
             block_diag(w["gate_x_w"], blk * per_block, per_block)], axis=1).astype(BF16)
    for row, name in enumerate(("gate_a_b_heads", "gate_x_b_heads")):
        heads = w[name]
        pieces = [_place_lanes(heads[h:h + 1, :], h * LRU_BLOCK, D_LRU) for h in range(LRU_HEADS)]
        gate_bias[row:row + 1, :] = functools.reduce(jnp.add, pieces)


def _block_kernel(tiles_per_seq, x_ref, p_ref, *rest):
    n_vmem, n_hbm = len(_VMEM_PARAMS), len(_HBM_WEIGHTS)
    w = dict(zip(_VMEM_PARAMS, rest[:n_vmem]))
    hbm = dict(zip(_HBM_WEIGHTS, rest[n_vmem:n_vmem + n_hbm]))
    o_ref = rest[n_vmem + n_hbm]
    scratch = rest[n_vmem + n_hbm + 1:]
    resident = dict(zip(_HBM_WEIGHTS, scratch[:n_hbm]))
    (stage, sems, pool_halo, conv_halo, h_carry, h1_buf,
     w_pool, w_gate, gate_bias, z_buf) = scratch[n_hbm:]
    w.update(resident)
    w.update(w_pool=w_pool, w_gate=w_gate,
             gate_a_b=gate_bias.at[0:1, :], gate_x_b=gate_bias.at[1:2, :])
    step = pl.program_id(0)
    t_rows = x_ref.shape[0]

    tile = step % tiles_per_seq

    @pl.when(tile == 0)
    def _():
        pool_halo[...] = jnp.zeros_like(pool_halo)
        conv_halo[...] = jnp.zeros_like(conv_halo)
        h_carry[...] = jnp.zeros_like(h_carry)

    def token_mixer():
        return _token_mixer(x_ref, tile * t_rows, w, pool_halo, conv_halo, h_carry)

    @pl.when(step == 0)
    def _():
        _assemble_head_params(w, w_pool, w_gate, gate_bias)
        _load_weights(hbm, resident, stage, sems)
        (h1_new, z_new), _ = _trace_stages(token_mixer(), iter(()), "")
        h1_buf[...] = h1_new
        z_buf[...] = z_new

    last_step = pl.num_programs(0) - 1

    @pl.when(jnp.logical_and(step > 0, step < last_step))
    def _():
        (h1_new, z_new), out = _trace_stages(
            token_mixer(), _channel_mixers(h1_buf, p_ref, w, z_buf), STAGE_ORDER)
        o_ref[...] = out
        h1_buf[...] = h1_new
        z_buf[...] = z_new

    @pl.when(step == last_step)
    def _():
        _, out = _trace_stages(iter(()), _channel_mixers(h1_buf, p_ref, w, z_buf), "")
        o_ref[...] = out


def _resident(shape):
    zeros = (0,) * len(shape)
    return pl.BlockSpec(shape, lambda s: zeros, pipeline_mode=pl.Buffered(1))


@jax.jit
def kernel(x, p, norm_mix_g, w_in, pool_w, pool_b, pool_scale, conv_w, conv_b, gate_a_w, gate_a_b, gate_x_w, gate_x_b, lru_L, w_out, norm_mlp_g, w_up, w_down, norm_ple_g, w_ple_gate, b_ple_gate, w_ple_proj, norm_final_g):
    batch, seq, d_model = x.shape
    assert d_model == D_MODEL and seq % SEQ_TILE == 0
    assert (SEQ_TILE // ROW_BLOCKS) % SUBLANES == 0 and SEQ_TILE // ROW_BLOCKS >= POOL_HALO
    assert p.shape[0] == 1 and w_in.shape[0] == 1, "single-layer block"
    tiles_per_seq = seq // SEQ_TILE
    n_tiles = batch * tiles_per_seq

    row = lambda v: v.reshape(1, -1).astype(F32)
    vmem_params = dict(
        g_mix=row(norm_mix_g[0]), pool_w=pool_w[0].astype(F32), pool_b=row(pool_b[0]),
        pool_scale=row(pool_scale[0]), conv_w=conv_w[0].astype(F32), conv_b=row(conv_b[0]),
        gate_a_w=gate_a_w[0].astype(F32), gate_x_w=gate_x_w[0].astype(F32),
        gate_a_b_heads=gate_a_b[0].astype(F32), gate_x_b_heads=gate_x_b[0].astype(F32),
        lru_l=row(lru_L[0]), g_mlp=row(norm_mlp_g[0]), g_ple=row(norm_ple_g[0]),
        b_ple_gate=row(b_ple_gate[0]), g_final=row(norm_final_g),
    )
    assert vmem_params["pool_w"].shape == (len(POOL_WINDOWS), POOL_GROUP_WIDTH, POOL_GROUP_WIDTH)
    assert vmem_params["gate_a_w"].shape == (LRU_HEADS, LRU_BLOCK, LRU_BLOCK)
    vmem_args = [vmem_params[name] for name in _VMEM_PARAMS]
    hbm_weights = dict(w_in=w_in, w_out=w_out, w_up=w_up, w_down=w_down,
                       w_ple_gate=w_ple_gate, w_ple_proj=w_ple_proj)
    hbm_args = [hbm_weights[name] for name in _HBM_WEIGHTS]
    for arg, (name, shape) in zip(hbm_args, _HBM_WEIGHTS.items()):
        assert arg.shape == (1,) + shape and arg.dtype == F32, name

    def tile_spec(width, lag):
        def index_map(s):
            t = jnp.clip(s - lag, 0, n_tiles - 1)
            return (t // tiles_per_seq, t % tiles_per_seq, 0)
        return pl.BlockSpec((None, SEQ_TILE, width), index_map)

    return pl.pallas_call(
        functools.partial(_block_kernel, tiles_per_seq),
        grid=(n_tiles + 1,),
        in_specs=[tile_spec(D_MODEL, 0), tile_spec(PLE_DIM, 1)]
                 + [_resident(a.shape) for a in vmem_args]
                 + [pl.BlockSpec(memory_space=pl.ANY)] * len(hbm_args),
        out_specs=tile_spec(D_MODEL, 1),
        out_shape=jax.ShapeDtypeStruct(x.shape, x.dtype),
        scratch_shapes=[pltpu.VMEM((n // DOT_N, k, DOT_N), BF16) for k, n in _HBM_WEIGHTS.values()]
        + [
            pltpu.VMEM((STAGE_SLOTS, STAGE_ROWS, DOT_N), F32),
            pltpu.SemaphoreType.DMA((STAGE_SLOTS,)),
            pltpu.VMEM((POOL_HALO, D_POOL), F32),
            pltpu.VMEM((CONV_HALO, D_LRU), F32),
            pltpu.VMEM((SUBLANES, D_LRU), F32),
            pltpu.VMEM((SEQ_TILE, D_MODEL), F32),
            pltpu.VMEM((D_POOL // MXU_DIM, MXU_DIM, MXU_DIM), BF16),
            pltpu.VMEM((D_LRU // MXU_DIM, MXU_DIM, 2 * MXU_DIM), BF16),
            pltpu.VMEM((SUBLANES, D_LRU), F32),
            pltpu.VMEM((SEQ_TILE, D_MODEL), BF16),
        ],
        compiler_params=pltpu.CompilerParams(
            dimension_semantics=("arbitrary",),
            vmem_limit_bytes=VMEM_LIMIT_BYTES,
        ),
        name="hybrid_block",
    )(x, p[0], *vmem_args, *hbm_args)
```

```python
import functools

import jax
import jax.numpy as jnp
from jax import lax
from jax.experimental import pallas as pl
from jax.experimental.pallas import tpu as pltpu

D_MODEL = 1024
D_POOL = 512
D_LRU = 512
POOL_WINDOWS = (2, 4, 8, 16)
POOL_GROUP_WIDTH = D_POOL // len(POOL_WINDOWS)
LRU_HEADS = 8
LRU_BLOCK = D_LRU // LRU_HEADS
CONV_WIDTH = 4
LRU_C = 8.0
D_FF = 4 * D_MODEL
PLE_DIM = 256
RMS_EPS = 1e-6

SUBLANES = 8
MXU_DIM = 256
VMEM_LIMIT_BYTES = 56 * 1024 * 1024

SEQ_TILE = 512
DOT_N = 2 * MXU_DIM
ROW_BLOCKS = 4
PLE_ROW_BLOCKS = 2
POOL_HALO = max(POOL_WINDOWS)
CONV_HALO = SUBLANES
STAGE_ROWS = 512
STAGE_SLOTS = 5
CAST_ROWS = 128

_FF_BLOCKS = D_FF // DOT_N
assert _FF_BLOCKS - _FF_BLOCKS // 2 >= ROW_BLOCKS
STAGE_ORDER = ("CT" + "C" * (_FF_BLOCKS // 2 - 1) + "T" * (2 * ROW_BLOCKS + 1)
               + "TTC" * ROW_BLOCKS + "C" * (_FF_BLOCKS - _FF_BLOCKS // 2 - ROW_BLOCKS)
               + "C" * _FF_BLOCKS + "T" + "C")

F32 = jnp.float32
BF16 = jnp.bfloat16


def _rms_norm(x, g):
    ms = jnp.mean(x * x, axis=-1, keepdims=True)
    return x * lax.rsqrt(ms + RMS_EPS) * g


def _dot(a, w_ref, blocks=None, rows=slice(None)):
    blocks = range(w_ref.shape[0]) if blocks is None else blocks
    outs = [jnp.dot(a, w_ref[j, rows, :], preferred_element_type=F32) for j in blocks]
    return outs[0] if len(outs) == 1 else jnp.concatenate(outs, axis=1)


def _head_dot(a, w_ref):
    return [jnp.dot(a[:, h * MXU_DIM:(h + 1) * MXU_DIM], w_ref[h], preferred_element_type=F32)
            for h in range(w_ref.shape[0])]


def _sigmoid(x):
    return 0.5 * jnp.tanh(0.5 * x) + 0.5


def _log_sigmoid(x):
    return -(jnp.maximum(-x, 0.0) + jnp.log1p(jnp.exp(-jnp.abs(x))))


def _linear_recurrence(a, b, h_prev):
    t_rows, width = a.shape
    sub = lax.broadcasted_iota(jnp.int32, (SUBLANES, width), 0)
    groups = []
    for k in range(t_rows // SUBLANES):
        av = a[k * SUBLANES:(k + 1) * SUBLANES]
        bv = b[k * SUBLANES:(k + 1) * SUBLANES]
        shift = 1
        while shift < SUBLANES:
            keep = sub >= shift
            bv = jnp.where(keep, av * pltpu.roll(bv, shift, axis=0) + bv, bv)
            av = jnp.where(keep, av * pltpu.roll(av, shift, axis=0), av)
            shift *= 2
        h = av * h_prev + bv
        groups.append(h)
        h_prev = h[SUBLANES - 1:SUBLANES, :]
    return jnp.concatenate(groups, axis=0), h_prev


def _token_mixer(x_ref, row0, w, pool_halo, conv_halo, h_carry):
    t_rows = x_ref.shape[0]
    row_blocks = [slice(k * t_rows // ROW_BLOCKS, (k + 1) * t_rows // ROW_BLOCKS)
                  for k in range(ROW_BLOCKS)]

    z = _rms_norm(x_ref[...], w["g_mix"][...]).astype(BF16)
    u_pool = _dot(z, w["w_in"], blocks=[0])
    u_lru = _dot(z, w["w_in"], blocks=[1])
    u_gate = _dot(z, w["w_in"], blocks=[2])
    yield

    t_idx = row0 + lax.broadcasted_iota(jnp.int32, (t_rows, 1), 0)

    ext = jnp.concatenate([pool_halo[...], u_pool], axis=0)
    pool_halo[...] = u_pool[t_rows - POOL_HALO:, :]
    d_blocks = []
    for rows in row_blocks:
        ext_rows = ext[rows.start:rows.stop + POOL_HALO]
        d_groups = []
        for g, win in enumerate(POOL_WINDOWS):
            ug = ext_rows[:, g * POOL_GROUP_WIDTH:(g + 1) * POOL_GROUP_WIDTH]
            s = ug
            k = 1
            while k < win:
                s = s + pltpu.roll(s, k, axis=0)
                k *= 2
            if rows.start < win:
                count = jnp.minimum(t_idx[rows] + 1, win).astype(F32)
                mean = s[POOL_HALO:] / count
            else:
                mean = s[POOL_HALO:] * (1.0 / win)
            d_groups.append(mean - ug[POOL_HALO:])
        d_blocks.append(jnp.concatenate(d_groups, axis=1).astype(BF16))
        yield
    d = jnp.concatenate(d_blocks, axis=0)
    y_pool = jnp.concatenate(_head_dot(d, w["w_pool"]), axis=1)
    y_pool = ((y_pool + w["pool_b"][...]) * w["pool_scale"][...]).astype(BF16)
    yield

    ext = jnp.concatenate([conv_halo[...], u_lru], axis=0)
    conv_halo[...] = u_lru[t_rows - CONV_HALO:, :]
    conv_w = w["conv_w"][...]
    xb_blocks = []
    for rows in row_blocks:
        ext_rows = ext[rows.start:rows.stop + CONV_HALO]
        xb = w["conv_b"][...]
        for k in range(CONV_WIDTH):
            lag = CONV_WIDTH - 1 - k
            tap = ext_rows if lag == 0 else pltpu.roll(ext_rows, lag, axis=0)
            xb = xb + tap * conv_w[k:k + 1, :]
        xb_blocks.append(xb[CONV_HALO:])
        yield
    xb = jnp.concatenate(xb_blocks, axis=0)

    xb_bf = xb.astype(BF16)
    log_sig_l = _log_sigmoid(w["lru_l"][...])
    y_blocks, h_last = [], h_carry[0:1, :]
    for rows in row_blocks:
        gate_parts = _head_dot(xb_bf[rows], w["w_gate"])
        r_pre = jnp.concatenate([g[:, :MXU_DIM] for g in gate_parts], axis=1)
        i_pre = jnp.concatenate([g[:, MXU_DIM:] for g in gate_parts], axis=1)
        r = jax.nn.sigmoid(r_pre + w["gate_a_b"][...])
        i = jax.nn.sigmoid(i_pre + w["gate_x_b"][...])
        a = jnp.exp(LRU_C * r * log_sig_l)
        one_minus_a2 = 1.0 - a * a
        mult = jnp.where(one_minus_a2 > 0.0, one_minus_a2 * lax.rsqrt(one_minus_a2), 0.0)
        mult = jnp.where(t_idx[rows] == 0, 1.0, mult)
        b = mult * (i * xb[rows])
        yield
        h, h_last = _linear_recurrence(a, b, h_last)
        y_blocks.append((h * jax.nn.gelu(u_gate[rows])).astype(BF16))
        yield
    h_carry[0:1, :] = h_last
    y_lru = jnp.concatenate(y_blocks, axis=0)

    h1 = x_ref[...] + (_dot(y_pool, w["w_out"], rows=slice(0, D_POOL))
                       + _dot(y_lru, w["w_out"], rows=slice(D_POOL, D_POOL + D_LRU)))
    return h1, _rms_norm(h1, w["g_mlp"][...]).astype(BF16)


def _channel_mixers(h1_ref, p_ref, w, z_buf):
    acts = []
    for c in range(D_FF // DOT_N):
        up = _dot(z_buf[...], w["w_up"], blocks=[c])
        act = jnp.maximum(up.astype(BF16), 0.0)
        acts.append(act * act)
        yield
    acc = None
    for c, act in enumerate(acts):
        down = _dot(act, w["w_down"], rows=slice(c * DOT_N, (c + 1) * DOT_N))
        acc = down if acc is None else acc + down
        yield
    h2 = h1_ref[...] + acc

    z = _rms_norm(h2, w["g_ple"][...]).astype(BF16)
    ple = _dot(p_ref[...].astype(BF16), w["w_ple_proj"])
    t_rows = z.shape[0]
    outs = []
    for k in range(PLE_ROW_BLOCKS):
        rows = slice(k * t_rows // PLE_ROW_BLOCKS, (k + 1) * t_rows // PLE_ROW_BLOCKS)
        gate = _sigmoid(_dot(z[rows], w["w_ple_gate"]) + w["b_ple_gate"][...])
        h3 = h2[rows] + gate * ple[rows]
        outs.append(_rms_norm(h3, w["g_final"][...]))
    return jnp.concatenate(outs, axis=0)


def _trace_stages(token_stages, channel_stages, order):
    generators = {"T": token_stages, "C": channel_stages}
    results = {}

    def advance(name):
        if name not in results:
            try:
                next(generators[name])
            except StopIteration as stop:
                results[name] = stop.value

    for name in order:
        advance(name)
    for name in generators:
        while name not in results:
            advance(name)
    return results["T"], results["C"]


_VMEM_PARAMS = (
    "g_mix", "pool_w", "pool_b", "pool_scale", "conv_w", "conv_b", "gate_a_w", "gate_x_w",
    "gate_a_b_heads", "gate_x_b_heads", "lru_l", "g_mlp", "g_ple", "b_ple_gate", "g_final",
)
_HBM_WEIGHTS = {
    "w_in": (D_MODEL, D_POOL + 2 * D_LRU),
    "w_out": (D_POOL + D_LRU, D_MODEL),
    "w_up": (D_MODEL, D_FF),
    "w_down": (D_FF, D_MODEL),
    "w_ple_gate": (D_MODEL, D_MODEL),
    "w_ple_proj": (PLE_DIM, D_MODEL),
}


def _load_weights(hbm, resident, stage, sems):
    pieces = []
    for name, (k_dim, n_dim) in _HBM_WEIGHTS.items():
        rows = min(k_dim, STAGE_ROWS)
        for j in range(n_dim // DOT_N):
            for r0 in range(0, k_dim, rows):
                pieces.append((hbm[name], r0, j * DOT_N, rows, resident[name], j))

    def copy(i):
        src, r0, c0, rows, _, _ = pieces[i]
        slot = i % STAGE_SLOTS
        return pltpu.make_async_copy(
            src.at[0, pl.ds(r0, rows), pl.ds(c0, DOT_N)],
            stage.at[slot, pl.ds(0, rows), :], sems.at[slot])

    lookahead = STAGE_SLOTS - 1
    for i in range(min(lookahead, len(pieces))):
        copy(i).start()
    for i, (_, r0, _, rows, dst, blk) in enumerate(pieces):
        if i + lookahead < len(pieces):
            copy(i + lookahead).start()
        copy(i).wait()
        slot = i % STAGE_SLOTS

        def cast_rows(k, carry, r0=r0, dst=dst, blk=blk, slot=slot):
            off = pl.multiple_of(k * CAST_ROWS, CAST_ROWS)
            dst[blk, pl.ds(pl.multiple_of(r0 + off, CAST_ROWS), CAST_ROWS), :] = (
                stage[slot, pl.ds(off, CAST_ROWS), :].astype(BF16))
            return carry

        lax.fori_loop(0, rows // CAST_ROWS, cast_rows, 0)


def _place_lanes(x, offset, width):
    return jnp.pad(x, ((0, 0), (offset, width - offset - x.shape[1])))


def _assemble_head_params(w, w_pool, w_gate, gate_bias):
    def block_diag(heads_ref, first, count):
        n = heads_ref.shape[-1]
        return jnp.concatenate(
            [_place_lanes(heads_ref[first + g], g * n, count * n) for g in range(count)], axis=0)

    per_block = MXU_DIM // POOL_GROUP_WIDTH
    for blk in range(w_pool.shape[0]):
        w_pool[blk] = block_diag(w["pool_w"], blk * per_block, per_block).astype(BF16)
    per_block = MXU_DIM // LRU_BLOCK
    for blk in range(w_gate.shape[0]):
        w_gate[blk] = jnp.concatenate(
            [block_diag(w["gate_a_w"], blk * per_block, per_block),
             block_diag(w["gate_x_w"], blk * per_block, per_block)], axis=1).astype(BF16)
    for row, name in enumerate(("gate_a_b_heads", "gate_x_b_heads")):
        heads = w[name]
        pieces = [_place_lanes(heads[h:h + 1, :], h * LRU_BLOCK, D_LRU) for h in range(LRU_HEADS)]
        gate_bias[row:row + 1, :] = functools.reduce(jnp.add, pieces)


def _block_kernel(tiles_per_seq, x_ref, p_ref, *rest):
    n_vmem, n_hbm = len(_VMEM_PARAMS), len(_HBM_WEIGHTS)
    w = dict(zip(_VMEM_PARAMS, rest[:n_vmem]))
    hbm = dict(zip(_HBM_WEIGHTS, rest[n_vmem:n_vmem + n_hbm]))
    o_ref = rest[n_vmem + n_hbm]
    scratch = rest[n_vmem + n_hbm + 1:]
    resident = dict(zip(_HBM_WEIGHTS, scratch[:n_hbm]))
    (stage, sems, pool_halo, conv_halo, h_carry, h1_buf,
     w_pool, w_gate, gate_bias, z_buf) = scratch[n_hbm:]
    w.update(resident)
    w.update(w_pool=w_pool, w_gate=w_gate,
             gate_a_b=gate_bias.at[0:1, :], gate_x_b=gate_bias.at[1:2, :])
    step = pl.program_id(0)
    t_rows = x_ref.shape[0]

    tile = step % tiles_per_seq

    @pl.when(tile == 0)
    def _():
        pool_halo[...] = jnp.zeros_like(pool_halo)
        conv_halo[...] = jnp.zeros_like(conv_halo)
        h_carry[...] = jnp.zeros_like(h_carry)

    def token_mixer():
        return _token_mixer(x_ref, tile * t_rows, w, pool_halo, conv_halo, h_carry)

    @pl.when(step == 0)
    def _():
        _assemble_head_params(w, w_pool, w_gate, gate_bias)
        _load_weights(hbm, resident, stage, sems)
        (h1_new, z_new), _ = _trace_stages(token_mixer(), iter(()), "")
        h1_buf[...] = h1_new
        z_buf[...] = z_new

    last_step = pl.num_programs(0) - 1

    @pl.when(jnp.logical_and(step > 0, step < last_step))
    def _():
        (h1_new, z_new), out = _trace_stages(
            token_mixer(), _channel_mixers(h1_buf, p_ref, w, z_buf), STAGE_ORDER)
        o_ref[...] = out
        h1_buf[...] = h1_new
        z_buf[...] = z_new

    @pl.when(step == last_step)
    def _():
        _, out = _trace_stages(iter(()), _channel_mixers(h1_buf, p_ref, w, z_buf), "")
        o_ref[...] = out


def _resident(shape):
    zeros = (0,) * len(shape)
    return pl.BlockSpec(shape, lambda s: zeros, pipeline_mode=pl.Buffered(1))


@jax.jit
def kernel(x, p, norm_mix_g, w_in, pool_w, pool_b, pool_scale, conv_w, conv_b, gate_a_w, gate_a_b, gate_x_w, gate_x_b, lru_L, w_out, norm_mlp_g, w_up, w_down, norm_ple_g, w_ple_gate, b_ple_gate, w_ple_proj, norm_final_g):
    batch, seq, d_model = x.shape
    assert d_model == D_MODEL and seq % SEQ_TILE == 0
    assert (SEQ_TILE // ROW_BLOCKS) % SUBLANES == 0 and SEQ_TILE // ROW_BLOCKS >= POOL_HALO
    assert p.shape[0] == 1 and w_in.shape[0] == 1, "single-layer block"
    tiles_per_seq = seq // SEQ_TILE
    n_tiles = batch * tiles_per_seq

    row = lambda v: v.reshape(1, -1).astype(F32)
    vmem_params = dict(
        g_mix=row(norm_mix_g[0]), pool_w=pool_w[0].astype(F32), pool_b=row(pool_b[0]),
        pool_scale=row(pool_scale[0]), conv_w=conv_w[0].astype(F32), conv_b=row(conv_b[0]),
        gate_a_w=gate_a_w[0].astype(F32), gate_x_w=gate_x_w[0].astype(F32),
        gate_a_b_heads=gate_a_b[0].astype(F32), gate_x_b_heads=gate_x_b[0].astype(F32),
        lru_l=row(lru_L[0]), g_mlp=row(norm_mlp_g[0]), g_ple=row(norm_ple_g[0]),
        b_ple_gate=row(b_ple_gate[0]), g_final=row(norm_final_g),
    )
    assert vmem_params["pool_w"].shape == (len(POOL_WINDOWS), POOL_GROUP_WIDTH, POOL_GROUP_WIDTH)
    assert vmem_params["gate_a_w"].shape == (LRU_HEADS, LRU_BLOCK, LRU_BLOCK)
    vmem_args = [vmem_params[name] for name in _VMEM_PARAMS]
    hbm_weights = dict(w_in=w_in, w_out=w_out, w_up=w_up, w_down=w_down,
                       w_ple_gate=w_ple_gate, w_ple_proj=w_ple_proj)
    hbm_args = [hbm_weights[name] for name in _HBM_WEIGHTS]
    for arg, (name, shape) in zip(hbm_args, _HBM_WEIGHTS.items()):
        assert arg.shape == (1,) + shape and arg.dtype == F32, name

    def tile_spec(width, lag):
        def index_map(s):
            t = jnp.clip(s - lag, 0, n_tiles - 1)
            return (t // tiles_per_seq, t % tiles_per_seq, 0)
        return pl.BlockSpec((None, SEQ_TILE, width), index_map)

    return pl.pallas_call(
        functools.partial(_block_kernel, tiles_per_seq),
        grid=(n_tiles + 1,),
        in_specs=[tile_spec(D_MODEL, 0), tile_spec(PLE_DIM, 1)]
                 + [_resident(a.shape) for a in vmem_args]
                 + [pl.BlockSpec(memory_space=pl.ANY)] * len(hbm_args),
        out_specs=tile_spec(D_MODEL, 1),
        out_shape=jax.ShapeDtypeStruct(x.shape, x.dtype),
        scratch_shapes=[pltpu.VMEM((n // DOT_N, k, DOT_N), BF16) for k, n in _HBM_WEIGHTS.values()]
        + [
            pltpu.VMEM((STAGE_SLOTS, STAGE_ROWS, DOT_N), F32),
            pltpu.SemaphoreType.DMA((STAGE_SLOTS,)),
            pltpu.VMEM((POOL_HALO, D_POOL), F32),
            pltpu.VMEM((CONV_HALO, D_LRU), F32),
            pltpu.VMEM((SUBLANES, D_LRU), F32),
            pltpu.VMEM((SEQ_TILE, D_MODEL), F32),
            pltpu.VMEM((D_POOL // MXU_DIM, MXU_DIM, MXU_DIM), BF16),
            pltpu.VMEM((D_LRU // MXU_DIM, MXU_DIM, 2 * MXU_DIM), BF16),
            pltpu.VMEM((SUBLANES, D_LRU), F32),
            pltpu.VMEM((SEQ_TILE, D_MODEL), BF16),
        ],
        compiler_params=pltpu.CompilerParams(
            dimension_semantics=("arbitrary",),
            vmem_limit_bytes=VMEM_LIMIT_BYTES,
        ),
        name="hybrid_block",
    )(x, p[0], *vmem_args, *hbm_args)
```

```python
import functools

import jax
import jax.numpy as jnp
from jax import lax
from jax.experimental import pallas as pl
from jax.experimental.pallas import tpu as pltpu

D_MODEL = 1024
D_POOL = 512
D_LRU = 512
POOL_WINDOWS = (2, 4, 8, 16)
POOL_GROUP_WIDTH = D_POOL // len(POOL_WINDOWS)
LRU_HEADS = 8
LRU_BLOCK = D_LRU // LRU_HEADS
CONV_WIDTH = 4
LRU_C = 8.0
D_FF = 4 * D_MODEL
PLE_DIM = 256
RMS_EPS = 1e-6

SUBLANES = 8
MXU_DIM = 256
VMEM_LIMIT_BYTES = 56 * 1024 * 1024

SEQ_TILE = 512
DOT_N = 2 * MXU_DIM
ROW_BLOCKS = 4
PLE_ROW_BLOCKS = 2
POOL_HALO = max(POOL_WINDOWS)
CONV_HALO = SUBLANES
STAGE_ROWS = 512
STAGE_SLOTS = 8
CAST_ROWS = 128

_FF_BLOCKS = D_FF // DOT_N
assert _FF_BLOCKS - _FF_BLOCKS // 2 >= ROW_BLOCKS
STAGE_ORDER = ("CT" + "C" * (_FF_BLOCKS // 2 - 1) + "T" * (2 * ROW_BLOCKS + 1)
               + "TTC" * ROW_BLOCKS + "C" * (_FF_BLOCKS - _FF_BLOCKS // 2 - ROW_BLOCKS)
               + "C" * _FF_BLOCKS + "T" + "C")

F32 = jnp.float32
BF16 = jnp.bfloat16


def _rms_norm(x, g):
    ms = jnp.mean(x * x, axis=-1, keepdims=True)
    return x * lax.rsqrt(ms + RMS_EPS) * g


def _dot(a, w_ref, blocks=None, rows=slice(None)):
    blocks = range(w_ref.shape[0]) if blocks is None else blocks
    outs = [jnp.dot(a, w_ref[j, rows, :], preferred_element_type=F32) for j in blocks]
    return outs[0] if len(outs) == 1 else jnp.concatenate(outs, axis=1)


def _head_dot(a, w_ref):
    return [jnp.dot(a[:, h * MXU_DIM:(h + 1) * MXU_DIM], w_ref[h], preferred_element_type=F32)
            for h in range(w_ref.shape[0])]


def _sigmoid(x):
    return 0.5 * jnp.tanh(0.5 * x) + 0.5


def _log_sigmoid(x):
    return -(jnp.maximum(-x, 0.0) + jnp.log1p(jnp.exp(-jnp.abs(x))))


def _linear_recurrence(a, b, h_prev):
    t_rows, width = a.shape
    sub = lax.broadcasted_iota(jnp.int32, (SUBLANES, width), 0)
    groups = []
    for k in range(t_rows // SUBLANES):
        av = a[k * SUBLANES:(k + 1) * SUBLANES]
        bv = b[k * SUBLANES:(k + 1) * SUBLANES]
        shift = 1
        while shift < SUBLANES:
            keep = sub >= shift
            bv = jnp.where(keep, av * pltpu.roll(bv, shift, axis=0) + bv, bv)
            av = jnp.where(keep, av * pltpu.roll(av, shift, axis=0), av)
            shift *= 2
        h = av * h_prev + bv
        groups.append(h)
        h_prev = h[SUBLANES - 1:SUBLANES, :]
    return jnp.concatenate(groups, axis=0), h_prev


def _token_mixer(x_ref, row0, w, pool_halo, conv_halo, h_carry):
    t_rows = x_ref.shape[0]
    row_blocks = [slice(k * t_rows // ROW_BLOCKS, (k + 1) * t_rows // ROW_BLOCKS)
                  for k in range(ROW_BLOCKS)]

    z = _rms_norm(x_ref[...], w["g_mix"][...]).astype(BF16)
    u_pool = _dot(z, w["w_in"], blocks=[0])
    u_lru = _dot(z, w["w_in"], blocks=[1])
    u_gate = _dot(z, w["w_in"], blocks=[2])
    yield

    t_idx = row0 + lax.broadcasted_iota(jnp.int32, (t_rows, 1), 0)

    ext = jnp.concatenate([pool_halo[...], u_pool], axis=0)
    pool_halo[...] = u_pool[t_rows - POOL_HALO:, :]
    d_blocks = []
    for rows in row_blocks:
        ext_rows = ext[rows.start:rows.stop + POOL_HALO]
        d_groups = []
        for g, win in enumerate(POOL_WINDOWS):
            ug = ext_rows[:, g * POOL_GROUP_WIDTH:(g + 1) * POOL_GROUP_WIDTH]
            s = ug
            k = 1
            while k < win:
                s = s + pltpu.roll(s, k, axis=0)
                k *= 2
            if rows.start < win:
                count = jnp.minimum(t_idx[rows] + 1, win).astype(F32)
                mean = s[POOL_HALO:] / count
            else:
                mean = s[POOL_HALO:] * (1.0 / win)
            d_groups.append(mean - ug[POOL_HALO:])
        d_blocks.append(jnp.concatenate(d_groups, axis=1).astype(BF16))
        yield
    d = jnp.concatenate(d_blocks, axis=0)
    y_pool = jnp.concatenate(_head_dot(d, w["w_pool"]), axis=1)
    y_pool = ((y_pool + w["pool_b"][...]) * w["pool_scale"][...]).astype(BF16)
    yield

    ext = jnp.concatenate([conv_halo[...], u_lru], axis=0)
    conv_halo[...] = u_lru[t_rows - CONV_HALO:, :]
    conv_w = w["conv_w"][...]
    xb_blocks = []
    for rows in row_blocks:
        ext_rows = ext[rows.start:rows.stop + CONV_HALO]
        xb = w["conv_b"][...]
        for k in range(CONV_WIDTH):
            lag = CONV_WIDTH - 1 - k
            tap = ext_rows if lag == 0 else pltpu.roll(ext_rows, lag, axis=0)
            xb = xb + tap * conv_w[k:k + 1, :]
        xb_blocks.append(xb[CONV_HALO:])
        yield
    xb = jnp.concatenate(xb_blocks, axis=0)

    xb_bf = xb.astype(BF16)
    log_sig_l = _log_sigmoid(w["lru_l"][...])
    y_blocks, h_last = [], h_carry[0:1, :]
    for rows in row_blocks:
        gate_parts = _head_dot(xb_bf[rows], w["w_gate"])
        r_pre = jnp.concatenate([g[:, :MXU_DIM] for g in gate_parts], axis=1)
        i_pre = jnp.concatenate([g[:, MXU_DIM:] for g in gate_parts], axis=1)
        r = jax.nn.sigmoid(r_pre + w["gate_a_b"][...])
        i = jax.nn.sigmoid(i_pre + w["gate_x_b"][...])
        a = jnp.exp(LRU_C * r * log_sig_l)
        one_minus_a2 = 1.0 - a * a
        mult = jnp.where(one_minus_a2 > 0.0, one_minus_a2 * lax.rsqrt(one_minus_a2), 0.0)
        mult = jnp.where(t_idx[rows] == 0, 1.0, mult)
        b = mult * (i * xb[rows])
        yield
        h, h_last = _linear_recurrence(a, b, h_last)
        y_blocks.append((h * jax.nn.gelu(u_gate[rows])).astype(BF16))
        yield
    h_carry[0:1, :] = h_last
    y_lru = jnp.concatenate(y_blocks, axis=0)

    h1 = x_ref[...] + (_dot(y_pool, w["w_out"], rows=slice(0, D_POOL))
                       + _dot(y_lru, w["w_out"], rows=slice(D_POOL, D_POOL + D_LRU)))
    return h1, _rms_norm(h1, w["g_mlp"][...]).astype(BF16)


def _channel_mixers(h1_ref, p_ref, w, z_buf):
    acts = []
    for c in range(D_FF // DOT_N):
        up = _dot(z_buf[...], w["w_up"], blocks=[c])
        act = jnp.maximum(up.astype(BF16), 0.0)
        acts.append(act * act)
        yield
    acc = None
    for c, act in enumerate(acts):
        down = _dot(act, w["w_down"], rows=slice(c * DOT_N, (c + 1) * DOT_N))
        acc = down if acc is None else acc + down
        yield
    h2 = h1_ref[...] + acc

    z = _rms_norm(h2, w["g_ple"][...]).astype(BF16)
    ple = _dot(p_ref[...].astype(BF16), w["w_ple_proj"])
    t_rows = z.shape[0]
    outs = []
    for k in range(PLE_ROW_BLOCKS):
        rows = slice(k * t_rows // PLE_ROW_BLOCKS, (k + 1) * t_rows // PLE_ROW_BLOCKS)
        gate = _sigmoid(_dot(z[rows], w["w_ple_gate"]) + w["b_ple_gate"][...])
        h3 = h2[rows] + gate * ple[rows]
        outs.append(_rms_norm(h3, w["g_final"][...]))
    return jnp.concatenate(outs, axis=0)


def _trace_stages(token_stages, channel_stages, order):
    generators = {"T": token_stages, "C": channel_stages}
    results = {}

    def advance(name):
        if name not in results:
            try:
                next(generators[name])
            except StopIteration as stop:
                results[name] = stop.value

    for name in order:
        advance(name)
    for name in generators:
        while name not in results:
            advance(name)
    return results["T"], results["C"]


_VMEM_PARAMS = (
    "g_mix", "pool_w", "pool_b", "pool_scale", "conv_w", "conv_b", "gate_a_w", "gate_x_w",
    "gate_a_b_heads", "gate_x_b_heads", "lru_l", "g_mlp", "g_ple", "b_ple_gate", "g_final",
)
_HBM_WEIGHTS = {
    "w_in": (D_MODEL, D_POOL + 2 * D_LRU),
    "w_out": (D_POOL + D_LRU, D_MODEL),
    "w_up": (D_MODEL, D_FF),
    "w_down": (D_FF, D_MODEL),
    "w_ple_gate": (D_MODEL, D_MODEL),
    "w_ple_proj": (PLE_DIM, D_MODEL),
}


def _load_weights(hbm, resident, stage, sems):
    pieces = []
    for name, (k_dim, n_dim) in _HBM_WEIGHTS.items():
        rows = min(k_dim, STAGE_ROWS)
        for j in range(n_dim // DOT_N):
            for r0 in range(0, k_dim, rows):
                pieces.append((hbm[name], r0, j * DOT_N, rows, resident[name], j))

    def copy(i):
        src, r0, c0, rows, _, _ = pieces[i]
        slot = i % STAGE_SLOTS
        return pltpu.make_async_copy(
            src.at[0, pl.ds(r0, rows), pl.ds(c0, DOT_N)],
            stage.at[slot, pl.ds(0, rows), :], sems.at[slot])

    lookahead = STAGE_SLOTS - 1
    for i in range(min(lookahead, len(pieces))):
        copy(i).start()
    for i, (_, r0, _, rows, dst, blk) in enumerate(pieces):
        if i + lookahead < len(pieces):
            copy(i + lookahead).start()
        copy(i).wait()
        slot = i % STAGE_SLOTS

        def cast_rows(k, carry, r0=r0, dst=dst, blk=blk, slot=slot):
            off = pl.multiple_of(k * CAST_ROWS, CAST_ROWS)
            dst[blk, pl.ds(pl.multiple_of(r0 + off, CAST_ROWS), CAST_ROWS), :] = (
                stage[slot, pl.ds(off, CAST_ROWS), :].astype(BF16))
            return carry

        lax.fori_loop(0, rows // CAST_ROWS, cast_rows, 0)


def _place_lanes(x, offset, width):
    return jnp.pad(x, ((0, 0), (offset, width - offset - x.shape[1])))


def _assemble_head_params(w, w_pool, w_gate, gate_bias):
    def block_diag(heads_ref, first, count):
        n = heads_ref.shape[-1]
        return jnp.concatenate(
            [_place_lanes(heads_ref[first + g], g * n, count * n) for g in range(count)], axis=0)

    per_block = MXU_DIM // POOL_GROUP_WIDTH
    for blk in range(w_pool.shape[0]):
        w_pool[blk] = block_diag(w["pool_w"], blk * per_block, per_block).astype(BF16)
    per_block = MXU_DIM // LRU_BLOCK
    for blk in range(w_gate.shape[0]):
        w_gate[blk] = jnp.concatenate(
            [block_diag(w["gate_a_w"], blk * per_block, per_block),
             block_diag(w["gate_x_w"], blk * per_block, per_block)], axis=1).astype(BF16)
    for row, name in enumerate(("gate_a_b_heads", "gate_x_b_heads")):
        heads = w[name]
        pieces = [_place_lanes(heads[h:h + 1, :], h * LRU_BLOCK, D_LRU) for h in range(LRU_HEADS)]
        gate_bias[row:row + 1, :] = functools.reduce(jnp.add, pieces)


def _block_kernel(tiles_per_seq, x_ref, p_ref, *rest):
    n_vmem, n_hbm = len(_VMEM_PARAMS), len(_HBM_WEIGHTS)
    w = dict(zip(_VMEM_PARAMS, rest[:n_vmem]))
    hbm = dict(zip(_HBM_WEIGHTS, rest[n_vmem:n_vmem + n_hbm]))
    o_ref = rest[n_vmem + n_hbm]
    scratch = rest[n_vmem + n_hbm + 1:]
    resident = dict(zip(_HBM_WEIGHTS, scratch[:n_hbm]))
    (stage, sems, pool_halo, conv_halo, h_carry, h1_buf,
     w_pool, w_gate, gate_bias, z_buf) = scratch[n_hbm:]
    w.update(resident)
    w.update(w_pool=w_pool, w_gate=w_gate,
             gate_a_b=gate_bias.at[0:1, :], gate_x_b=gate_bias.at[1:2, :])
    step = pl.program_id(0)
    t_rows = x_ref.shape[0]

    tile = step % tiles_per_seq

    @pl.when(tile == 0)
    def _():
        pool_halo[...] = jnp.zeros_like(pool_halo)
        conv_halo[...] = jnp.zeros_like(conv_halo)
        h_carry[...] = jnp.zeros_like(h_carry)

    def token_mixer():
        return _token_mixer(x_ref, tile * t_rows, w, pool_halo, conv_halo, h_carry)

    @pl.when(step == 0)
    def _():
        _assemble_head_params(w, w_pool, w_gate, gate_bias)
        _load_weights(hbm, resident, stage, sems)
        (h1_new, z_new), _ = _trace_stages(token_mixer(), iter(()), "")
        h1_buf[...] = h1_new
        z_buf[...] = z_new

    last_step = pl.num_programs(0) - 1

    @pl.when(jnp.logical_and(step > 0, step < last_step))
    def _():
        (h1_new, z_new), out = _trace_stages(
            token_mixer(), _channel_mixers(h1_buf, p_ref, w, z_buf), STAGE_ORDER)
        o_ref[...] = out
        h1_buf[...] = h1_new
        z_buf[...] = z_new

    @pl.when(step == last_step)
    def _():
        _, out = _trace_stages(iter(()), _channel_mixers(h1_buf, p_ref, w, z_buf), "")
        o_ref[...] = out


def _resident(shape):
    zeros = (0,) * len(shape)
    return pl.BlockSpec(shape, lambda s: zeros, pipeline_mode=pl.Buffered(1))


@jax.jit
def kernel(x, p, norm_mix_g, w_in, pool_w, pool_b, pool_scale, conv_w, conv_b, gate_a_w, gate_a_b, gate_x_w, gate_x_b, lru_L, w_out, norm_mlp_g, w_up, w_down, norm_ple_g, w_ple_gate, b_ple_gate, w_ple_proj, norm_final_g):
    batch, seq, d_model = x.shape
    assert d_model == D_MODEL and seq % SEQ_TILE == 0
    assert (SEQ_TILE // ROW_BLOCKS) % SUBLANES == 0 and SEQ_TILE // ROW_BLOCKS >= POOL_HALO
    assert p.shape[0] == 1 and w_in.shape[0] == 1, "single-layer block"
    tiles_per_seq = seq // SEQ_TILE
    n_tiles = batch * tiles_per_seq

    row = lambda v: v.reshape(1, -1).astype(F32)
    vmem_params = dict(
        g_mix=row(norm_mix_g[0]), pool_w=pool_w[0].astype(F32), pool_b=row(pool_b[0]),
        pool_scale=row(pool_scale[0]), conv_w=conv_w[0].astype(F32), conv_b=row(conv_b[0]),
        gate_a_w=gate_a_w[0].astype(F32), gate_x_w=gate_x_w[0].astype(F32),
        gate_a_b_heads=gate_a_b[0].astype(F32), gate_x_b_heads=gate_x_b[0].astype(F32),
        lru_l=row(lru_L[0]), g_mlp=row(norm_mlp_g[0]), g_ple=row(norm_ple_g[0]),
        b_ple_gate=row(b_ple_gate[0]), g_final=row(norm_final_g),
    )
    assert vmem_params["pool_w"].shape == (len(POOL_WINDOWS), POOL_GROUP_WIDTH, POOL_GROUP_WIDTH)
    assert vmem_params["gate_a_w"].shape == (LRU_HEADS, LRU_BLOCK, LRU_BLOCK)
    vmem_args = [vmem_params[name] for name in _VMEM_PARAMS]
    hbm_weights = dict(w_in=w_in, w_out=w_out, w_up=w_up, w_down=w_down,
                       w_ple_gate=w_ple_gate, w_ple_proj=w_ple_proj)
    hbm_args = [hbm_weights[name] for name in _HBM_WEIGHTS]
    for arg, (name, shape) in zip(hbm_args, _HBM_WEIGHTS.items()):
        assert arg.shape == (1,) + shape and arg.dtype == F32, name

    def tile_spec(width, lag):
        def index_map(s):
            t = jnp.clip(s - lag, 0, n_tiles - 1)
            return (t // tiles_per_seq, t % tiles_per_seq, 0)
        return pl.BlockSpec((None, SEQ_TILE, width), index_map)

    return pl.pallas_call(
        functools.partial(_block_kernel, tiles_per_seq),
        grid=(n_tiles + 1,),
        in_specs=[tile_spec(D_MODEL, 0), tile_spec(PLE_DIM, 1)]
                 + [_resident(a.shape) for a in vmem_args]
                 + [pl.BlockSpec(memory_space=pl.ANY)] * len(hbm_args),
        out_specs=tile_spec(D_MODEL, 1),
        out_shape=jax.ShapeDtypeStruct(x.shape, x.dtype),
        scratch_shapes=[pltpu.VMEM((n // DOT_N, k, DOT_N), BF16) for k, n in _HBM_WEIGHTS.values()]
        + [
            pltpu.VMEM((STAGE_SLOTS, STAGE_ROWS, DOT_N), F32),
            pltpu.SemaphoreType.DMA((STAGE_SLOTS,)),
            pltpu.VMEM((POOL_HALO, D_POOL), F32),
            pltpu.VMEM((CONV_HALO, D_LRU), F32),
            pltpu.VMEM((SUBLANES, D_LRU), F32),
            pltpu.VMEM((SEQ_TILE, D_MODEL), F32),
            pltpu.VMEM((D_POOL // MXU_DIM, MXU_DIM, MXU_DIM), BF16),
            pltpu.VMEM((D_LRU // MXU_DIM, MXU_DIM, 2 * MXU_DIM), BF16),
            pltpu.VMEM((SUBLANES, D_LRU), F32),
            pltpu.VMEM((SEQ_TILE, D_MODEL), BF16),
        ],
        compiler_params=pltpu.CompilerParams(
            dimension_semantics=("arbitrary",),
            vmem_limit_bytes=VMEM_LIMIT_BYTES,
        ),
        name="hybrid_block",
    )(x, p[0], *vmem_args, *hbm_args)
```

```python
import functools

import jax
import jax.numpy as jnp
from jax import lax
from jax.experimental import pallas as pl
from jax.experimental.pallas import tpu as pltpu

D_MODEL = 1024
D_POOL = 512
D_LRU = 512
POOL_WINDOWS = (2, 4, 8, 16)
POOL_GROUP_WIDTH = D_POOL // len(POOL_WINDOWS)
LRU_HEADS = 8
LRU_BLOCK = D_LRU // LRU_HEADS
CONV_WIDTH = 4
LRU_C = 8.0
D_FF = 4 * D_MODEL
PLE_DIM = 256
RMS_EPS = 1e-6

SUBLANES = 8
MXU_DIM = 256
VMEM_LIMIT_BYTES = 56 * 1024 * 1024

SEQ_TILE = 512
DOT_N = 2 * MXU_DIM
ROW_BLOCKS = 4
PLE_ROW_BLOCKS = 2
POOL_HALO = max(POOL_WINDOWS)
CONV_HALO = SUBLANES
STAGE_ROWS = 512
STAGE_SLOTS = 8
CAST_ROWS = 128

_FF_BLOCKS = D_FF // DOT_N
assert _FF_BLOCKS - _FF_BLOCKS // 2 >= ROW_BLOCKS
STAGE_ORDER = ("CT" + "C" * (_FF_BLOCKS // 2 - 1) + "T" * (2 * ROW_BLOCKS + 1)
               + "TTC" * ROW_BLOCKS + "C" * (_FF_BLOCKS - _FF_BLOCKS // 2 - ROW_BLOCKS)
               + "C" * _FF_BLOCKS + "T" + "C")

TOKEN_STAGES_BEFORE_OUT_PROJ = 2 + 4 * ROW_BLOCKS

F32 = jnp.float32
BF16 = jnp.bfloat16


def _rms_norm(x, g):
    ms = jnp.mean(x * x, axis=-1, keepdims=True)
    return x * lax.rsqrt(ms + RMS_EPS) * g


def _dot(a, w_ref, blocks=None, rows=slice(None)):
    blocks = range(w_ref.shape[0]) if blocks is None else blocks
    outs = [jnp.dot(a, w_ref[j, rows, :], preferred_element_type=F32) for j in blocks]
    return outs[0] if len(outs) == 1 else jnp.concatenate(outs, axis=1)


def _head_dot(a, w_ref):
    return [jnp.dot(a[:, h * MXU_DIM:(h + 1) * MXU_DIM], w_ref[h], preferred_element_type=F32)
            for h in range(w_ref.shape[0])]


def _sigmoid(x):
    return 0.5 * jnp.tanh(0.5 * x) + 0.5


def _log_sigmoid(x):
    return -(jnp.maximum(-x, 0.0) + jnp.log1p(jnp.exp(-jnp.abs(x))))


def _linear_recurrence(a, b, h_prev):
    t_rows, width = a.shape
    sub = lax.broadcasted_iota(jnp.int32, (SUBLANES, width), 0)
    groups = []
    for k in range(t_rows // SUBLANES):
        av = a[k * SUBLANES:(k + 1) * SUBLANES]
        bv = b[k * SUBLANES:(k + 1) * SUBLANES]
        shift = 1
        while shift < SUBLANES:
            keep = sub >= shift
            bv = jnp.where(keep, av * pltpu.roll(bv, shift, axis=0) + bv, bv)
            av = jnp.where(keep, av * pltpu.roll(av, shift, axis=0), av)
            shift *= 2
        h = av * h_prev + bv
        groups.append(h)
        h_prev = h[SUBLANES - 1:SUBLANES, :]
    return jnp.concatenate(groups, axis=0), h_prev


def _token_mixer(x_ref, row0, w, pool_halo, conv_halo, h_carry):
    t_rows = x_ref.shape[0]
    row_blocks = [slice(k * t_rows // ROW_BLOCKS, (k + 1) * t_rows // ROW_BLOCKS)
                  for k in range(ROW_BLOCKS)]

    z = _rms_norm(x_ref[...], w["g_mix"][...]).astype(BF16)
    u_pool = _dot(z, w["w_in"], blocks=[0])
    u_lru = _dot(z, w["w_in"], blocks=[1])
    u_gate = _dot(z, w["w_in"], blocks=[2])
    yield

    t_idx = row0 + lax.broadcasted_iota(jnp.int32, (t_rows, 1), 0)

    ext = jnp.concatenate([pool_halo[...], u_pool], axis=0)
    pool_halo[...] = u_pool[t_rows - POOL_HALO:, :]
    d_blocks = []
    for rows in row_blocks:
        ext_rows = ext[rows.start:rows.stop + POOL_HALO]
        d_groups = []
        for g, win in enumerate(POOL_WINDOWS):
            ug = ext_rows[:, g * POOL_GROUP_WIDTH:(g + 1) * POOL_GROUP_WIDTH]
            s = ug
            k = 1
            while k < win:
                s = s + pltpu.roll(s, k, axis=0)
                k *= 2
            if rows.start < win:
                count = jnp.minimum(t_idx[rows] + 1, win).astype(F32)
                mean = s[POOL_HALO:] / count
            else:
                mean = s[POOL_HALO:] * (1.0 / win)
            d_groups.append(mean - ug[POOL_HALO:])
        d_blocks.append(jnp.concatenate(d_groups, axis=1).astype(BF16))
        yield
    d = jnp.concatenate(d_blocks, axis=0)
    y_pool = jnp.concatenate(_head_dot(d, w["w_pool"]), axis=1)
    y_pool = ((y_pool + w["pool_b"][...]) * w["pool_scale"][...]).astype(BF16)
    yield

    ext = jnp.concatenate([conv_halo[...], u_lru], axis=0)
    conv_halo[...] = u_lru[t_rows - CONV_HALO:, :]
    conv_w = w["conv_w"][...]
    xb_blocks = []
    for rows in row_blocks:
        ext_rows = ext[rows.start:rows.stop + CONV_HALO]
        xb = w["conv_b"][...]
        for k in range(CONV_WIDTH):
            lag = CONV_WIDTH - 1 - k
            tap = ext_rows if lag == 0 else pltpu.roll(ext_rows, lag, axis=0)
            xb = xb + tap * conv_w[k:k + 1, :]
        xb_blocks.append(xb[CONV_HALO:])
        yield
    xb = jnp.concatenate(xb_blocks, axis=0)

    xb_bf = xb.astype(BF16)
    log_sig_l = _log_sigmoid(w["lru_l"][...])
    y_blocks, h_last = [], h_carry[0:1, :]
    for rows in row_blocks:
        gate_parts = _head_dot(xb_bf[rows], w["w_gate"])
        r_pre = jnp.concatenate([g[:, :MXU_DIM] for g in gate_parts], axis=1)
        i_pre = jnp.concatenate([g[:, MXU_DIM:] for g in gate_parts], axis=1)
        r = jax.nn.sigmoid(r_pre + w["gate_a_b"][...])
        i = jax.nn.sigmoid(i_pre + w["gate_x_b"][...])
        a = jnp.exp(LRU_C * r * log_sig_l)
        one_minus_a2 = 1.0 - a * a
        mult = jnp.where(one_minus_a2 > 0.0, one_minus_a2 * lax.rsqrt(one_minus_a2), 0.0)
        mult = jnp.where(t_idx[rows] == 0, 1.0, mult)
        b = mult * (i * xb[rows])
        yield
        h, h_last = _linear_recurrence(a, b, h_last)
        y_blocks.append((h * jax.nn.gelu(u_gate[rows])).astype(BF16))
        yield
    h_carry[0:1, :] = h_last
    y_lru = jnp.concatenate(y_blocks, axis=0)

    h1 = x_ref[...] + (_dot(y_pool, w["w_out"], rows=slice(0, D_POOL))
                       + _dot(y_lru, w["w_out"], rows=slice(D_POOL, D_POOL + D_LRU)))
    return h1, _rms_norm(h1, w["g_mlp"][...]).astype(BF16)


def _channel_mixers(h1_ref, p_ref, w, z_buf):
    acts = []
    for c in range(D_FF // DOT_N):
        up = _dot(z_buf[...], w["w_up"], blocks=[c])
        act = jnp.maximum(up.astype(BF16), 0.0)
        acts.append(act * act)
        yield
    acc = None
    for c, act in enumerate(acts):
        down = _dot(act, w["w_down"], rows=slice(c * DOT_N, (c + 1) * DOT_N))
        acc = down if acc is None else acc + down
        yield
    h2 = h1_ref[...] + acc

    z = _rms_norm(h2, w["g_ple"][...]).astype(BF16)
    ple = _dot(p_ref[...].astype(BF16), w["w_ple_proj"])
    t_rows = z.shape[0]
    outs = []
    for k in range(PLE_ROW_BLOCKS):
        rows = slice(k * t_rows // PLE_ROW_BLOCKS, (k + 1) * t_rows // PLE_ROW_BLOCKS)
        gate = _sigmoid(_dot(z[rows], w["w_ple_gate"]) + w["b_ple_gate"][...])
        h3 = h2[rows] + gate * ple[rows]
        outs.append(_rms_norm(h3, w["g_final"][...]))
    return jnp.concatenate(outs, axis=0)


def _trace_stages(token_stages, channel_stages, order):
    generators = {"T": token_stages, "C": channel_stages}
    results = {}

    def advance(name):
        if name not in results:
            try:
                next(generators[name])
            except StopIteration as stop:
                results[name] = stop.value

    for name in order:
        advance(name)
    for name in generators:
        while name not in results:
            advance(name)
    return results["T"], results["C"]


_VMEM_PARAMS = (
    "g_mix", "pool_w", "pool_b", "pool_scale", "conv_w", "conv_b", "gate_a_w", "gate_x_w",
    "gate_a_b_heads", "gate_x_b_heads", "lru_l", "g_mlp", "g_ple", "b_ple_gate", "g_final",
)
_HBM_WEIGHTS = {
    "w_in": (D_MODEL, D_POOL + 2 * D_LRU),
    "w_out": (D_POOL + D_LRU, D_MODEL),
    "w_up": (D_MODEL, D_FF),
    "w_down": (D_FF, D_MODEL),
    "w_ple_gate": (D_MODEL, D_MODEL),
    "w_ple_proj": (PLE_DIM, D_MODEL),
}


def _load_weights(hbm, resident, stage, sems, first):
    names = list(first) + [name for name in _HBM_WEIGHTS if name not in first]
    pieces = []
    pause_after = set()
    for name in names:
        k_dim, n_dim = _HBM_WEIGHTS[name]
        rows = min(k_dim, STAGE_ROWS)
        for j in range(n_dim // DOT_N):
            for r0 in range(0, k_dim, rows):
                pieces.append((hbm[name], r0, j * DOT_N, rows, resident[name], j))
        if name in first:
            pause_after.add(len(pieces) - 1)

    def copy(i):
        src, r0, c0, rows, _, _ = pieces[i]
        slot = i % STAGE_SLOTS
        return pltpu.make_async_copy(
            src.at[0, pl.ds(r0, rows), pl.ds(c0, DOT_N)],
            stage.at[slot, pl.ds(0, rows), :], sems.at[slot])

    lookahead = STAGE_SLOTS - 1
    for i in range(min(lookahead, len(pieces))):
        copy(i).start()
    for i, (_, r0, _, rows, dst, blk) in enumerate(pieces):
        if i + lookahead < len(pieces):
            copy(i + lookahead).start()
        copy(i).wait()
        slot = i % STAGE_SLOTS

        def cast_rows(k, carry, r0=r0, dst=dst, blk=blk, slot=slot):
            off = pl.multiple_of(k * CAST_ROWS, CAST_ROWS)
            dst[blk, pl.ds(pl.multiple_of(r0 + off, CAST_ROWS), CAST_ROWS), :] = (
                stage[slot, pl.ds(off, CAST_ROWS), :].astype(BF16))
            return carry

        lax.fori_loop(0, rows // CAST_ROWS, cast_rows, 0)
        if i in pause_after:
            yield


def _place_lanes(x, offset, width):
    return jnp.pad(x, ((0, 0), (offset, width - offset - x.shape[1])))


def _assemble_head_params(w, w_pool, w_gate, gate_bias):
    def block_diag(heads_ref, first, count):
        n = heads_ref.shape[-1]
        return jnp.concatenate(
            [_place_lanes(heads_ref[first + g], g * n, count * n) for g in range(count)], axis=0)

    per_block = MXU_DIM // POOL_GROUP_WIDTH
    for blk in range(w_pool.shape[0]):
        w_pool[blk] = block_diag(w["pool_w"], blk * per_block, per_block).astype(BF16)
    per_block = MXU_DIM // LRU_BLOCK
    for blk in range(w_gate.shape[0]):
        w_gate[blk] = jnp.concatenate(
            [block_diag(w["gate_a_w"], blk * per_block, per_block),
             block_diag(w["gate_x_w"], blk * per_block, per_block)], axis=1).astype(BF16)
    for row, name in enumerate(("gate_a_b_heads", "gate_x_b_heads")):
        heads = w[name]
        pieces = [_place_lanes(heads[h:h + 1, :], h * LRU_BLOCK, D_LRU) for h in range(LRU_HEADS)]
        gate_bias[row:row + 1, :] = functools.reduce(jnp.add, pieces)


def _block_kernel(tiles_per_seq, x_ref, p_ref, *rest):
    n_vmem, n_hbm = len(_VMEM_PARAMS), len(_HBM_WEIGHTS)
    w = dict(zip(_VMEM_PARAMS, rest[:n_vmem]))
    hbm = dict(zip(_HBM_WEIGHTS, rest[n_vmem:n_vmem + n_hbm]))
    o_ref = rest[n_vmem + n_hbm]
    scratch = rest[n_vmem + n_hbm + 1:]
    resident = dict(zip(_HBM_WEIGHTS, scratch[:n_hbm]))
    (stage, sems, pool_halo, conv_halo, h_carry, h1_buf,
     w_pool, w_gate, gate_bias, z_buf) = scratch[n_hbm:]
    w.update(resident)
    w.update(w_pool=w_pool, w_gate=w_gate,
             gate_a_b=gate_bias.at[0:1, :], gate_x_b=gate_bias.at[1:2, :])
    step = pl.program_id(0)
    t_rows = x_ref.shape[0]

    tile = step % tiles_per_seq

    @pl.when(tile == 0)
    def _():
        pool_halo[...] = jnp.zeros_like(pool_halo)
        conv_halo[...] = jnp.zeros_like(conv_halo)
        h_carry[...] = jnp.zeros_like(h_carry)

    def token_mixer():
        return _token_mixer(x_ref, tile * t_rows, w, pool_halo, conv_halo, h_carry)

    @pl.when(step == 0)
    def _():
        _assemble_head_params(w, w_pool, w_gate, gate_bias)
        loader = _load_weights(hbm, resident, stage, sems, first=("w_in", "w_out"))
        next(loader)
        mixer = token_mixer()
        for _ in range(TOKEN_STAGES_BEFORE_OUT_PROJ):
            next(mixer)
        next(loader)
        (h1_new, z_new), _ = _trace_stages(mixer, iter(()), "")
        h1_buf[...] = h1_new
        z_buf[...] = z_new
        for _ in loader:
            pass

    last_step = pl.num_programs(0) - 1

    @pl.when(jnp.logical_and(step > 0, step < last_step))
    def _():
        (h1_new, z_new), out = _trace_stages(
            token_mixer(), _channel_mixers(h1_buf, p_ref, w, z_buf), STAGE_ORDER)
        o_ref[...] = out
        h1_buf[...] = h1_new
        z_buf[...] = z_new

    @pl.when(step == last_step)
    def _():
        _, out = _trace_stages(iter(()), _channel_mixers(h1_buf, p_ref, w, z_buf), "")
        o_ref[...] = out


def _resident(shape):
    zeros = (0,) * len(shape)
    return pl.BlockSpec(shape, lambda s: zeros, pipeline_mode=pl.Buffered(1))


@jax.jit
def kernel(x, p, norm_mix_g, w_in, pool_w, pool_b, pool_scale, conv_w, conv_b, gate_a_w, gate_a_b, gate_x_w, gate_x_b, lru_L, w_out, norm_mlp_g, w_up, w_down, norm_ple_g, w_ple_gate, b_ple_gate, w_ple_proj, norm_final_g):
    batch, seq, d_model = x.shape
    assert d_model == D_MODEL and seq % SEQ_TILE == 0
    assert (SEQ_TILE // ROW_BLOCKS) % SUBLANES == 0 and SEQ_TILE // ROW_BLOCKS >= POOL_HALO
    assert p.shape[0] == 1 and w_in.shape[0] == 1, "single-layer block"
    tiles_per_seq = seq // SEQ_TILE
    n_tiles = batch * tiles_per_seq

    row = lambda v: v.reshape(1, -1).astype(F32)
    vmem_params = dict(
        g_mix=row(norm_mix_g[0]), pool_w=pool_w[0].astype(F32), pool_b=row(pool_b[0]),
        pool_scale=row(pool_scale[0]), conv_w=conv_w[0].astype(F32), conv_b=row(conv_b[0]),
        gate_a_w=gate_a_w[0].astype(F32), gate_x_w=gate_x_w[0].astype(F32),
        gate_a_b_heads=gate_a_b[0].astype(F32), gate_x_b_heads=gate_x_b[0].astype(F32),
        lru_l=row(lru_L[0]), g_mlp=row(norm_mlp_g[0]), g_ple=row(norm_ple_g[0]),
        b_ple_gate=row(b_ple_gate[0]), g_final=row(norm_final_g),
    )
    assert vmem_params["pool_w"].shape == (len(POOL_WINDOWS), POOL_GROUP_WIDTH, POOL_GROUP_WIDTH)
    assert vmem_params["gate_a_w"].shape == (LRU_HEADS, LRU_BLOCK, LRU_BLOCK)
    vmem_args = [vmem_params[name] for name in _VMEM_PARAMS]
    hbm_weights = dict(w_in=w_in, w_out=w_out, w_up=w_up, w_down=w_down,
                       w_ple_gate=w_ple_gate, w_ple_proj=w_ple_proj)
    hbm_args = [hbm_weights[name] for name in _HBM_WEIGHTS]
    for arg, (name, shape) in zip(hbm_args, _HBM_WEIGHTS.items()):
        assert arg.shape == (1,) + shape and arg.dtype == F32, name

    def tile_spec(width, lag):
        def index_map(s):
            t = jnp.clip(s - lag, 0, n_tiles - 1)
            return (t // tiles_per_seq, t % tiles_per_seq, 0)
        return pl.BlockSpec((None, SEQ_TILE, width), index_map)

    return pl.pallas_call(
        functools.partial(_block_kernel, tiles_per_seq),
        grid=(n_tiles + 1,),
        in_specs=[tile_spec(D_MODEL, 0), tile_spec(PLE_DIM, 1)]
                 + [_resident(a.shape) for a in vmem_args]
                 + [pl.BlockSpec(memory_space=pl.ANY)] * len(hbm_args),
        out_specs=tile_spec(D_MODEL, 1),
        out_shape=jax.ShapeDtypeStruct(x.shape, x.dtype),
        scratch_shapes=[pltpu.VMEM((n // DOT_N, k, DOT_N), BF16) for k, n in _HBM_WEIGHTS.values()]
        + [
            pltpu.VMEM((STAGE_SLOTS, STAGE_ROWS, DOT_N), F32),
            pltpu.SemaphoreType.DMA((STAGE_SLOTS,)),
            pltpu.VMEM((POOL_HALO, D_POOL), F32),
            pltpu.VMEM((CONV_HALO, D_LRU), F32),
            pltpu.VMEM((SUBLANES, D_LRU), F32),
            pltpu.VMEM((SEQ_TILE, D_MODEL), F32),
            pltpu.VMEM((D_POOL // MXU_DIM, MXU_DIM, MXU_DIM), BF16),
            pltpu.VMEM((D_LRU // MXU_DIM, MXU_DIM, 2 * MXU_DIM), BF16),
            pltpu.VMEM((SUBLANES, D_LRU), F32),
            pltpu.VMEM((SEQ_TILE, D_MODEL), BF16),
        ],
        compiler_params=pltpu.CompilerParams(
            dimension_semantics=("arbitrary",),
            vmem_limit_bytes=VMEM_LIMIT_BYTES,
        ),
        name="hybrid_block",
    )(x, p[0], *vmem_args, *hbm_args)
```

```python
import functools

import jax
import jax.numpy as jnp
from jax import lax
from jax.experimental import pallas as pl
from jax.experimental.pallas import tpu as pltpu

D_MODEL = 1024
D_POOL = 512
D_LRU = 512
POOL_WINDOWS = (2, 4, 8, 16)
POOL_GROUP_WIDTH = D_POOL // len(POOL_WINDOWS)
LRU_HEADS = 8
LRU_BLOCK = D_LRU // LRU_HEADS
CONV_WIDTH = 4
LRU_C = 8.0
D_FF = 4 * D_MODEL
PLE_DIM = 256
RMS_EPS = 1e-6

SUBLANES = 8
MXU_DIM = 256
VMEM_LIMIT_BYTES = 60 * 1024 * 1024

SEQ_TILE = 512
DOT_N = 2 * MXU_DIM
ROW_BLOCKS = 4
PLE_ROW_BLOCKS = 2
POOL_HALO = max(POOL_WINDOWS)
CONV_HALO = SUBLANES
STAGE_ROWS = 512
STAGE_SLOTS = 8
CAST_ROWS = 128

_FF_BLOCKS = D_FF // DOT_N
assert _FF_BLOCKS - _FF_BLOCKS // 2 >= ROW_BLOCKS
STAGE_ORDER = ("CT" + "C" * (_FF_BLOCKS // 2 - 1) + "T" * (2 * ROW_BLOCKS + 1)
               + "TTC" * ROW_BLOCKS + "C" * (_FF_BLOCKS - _FF_BLOCKS // 2 - ROW_BLOCKS)
               + "C" * _FF_BLOCKS + "T" + "C")

F32 = jnp.float32
BF16 = jnp.bfloat16


def _rms_norm(x, g):
    ms = jnp.mean(x * x, axis=-1, keepdims=True)
    return x * lax.rsqrt(ms + RMS_EPS) * g


def _dot(a, w_ref, blocks=None, rows=slice(None)):
    blocks = range(w_ref.shape[0]) if blocks is None else blocks
    outs = [jnp.dot(a, w_ref[j, rows, :], preferred_element_type=F32) for j in blocks]
    return outs[0] if len(outs) == 1 else jnp.concatenate(outs, axis=1)


def _head_dot(a, w_ref):
    return [jnp.dot(a[:, h * MXU_DIM:(h + 1) * MXU_DIM], w_ref[h], preferred_element_type=F32)
            for h in range(w_ref.shape[0])]


def _sigmoid(x):
    return 0.5 * jnp.tanh(0.5 * x) + 0.5


def _log_sigmoid(x):
    return -(jnp.maximum(-x, 0.0) + jnp.log1p(jnp.exp(-jnp.abs(x))))


def _linear_recurrence(a, b, h_prev):
    t_rows, width = a.shape
    sub = lax.broadcasted_iota(jnp.int32, (SUBLANES, width), 0)
    groups = []
    for k in range(t_rows // SUBLANES):
        av = a[k * SUBLANES:(k + 1) * SUBLANES]
        bv = b[k * SUBLANES:(k + 1) * SUBLANES]
        shift = 1
        while shift < SUBLANES:
            keep = sub >= shift
            bv = jnp.where(keep, av * pltpu.roll(bv, shift, axis=0) + bv, bv)
            av = jnp.where(keep, av * pltpu.roll(av, shift, axis=0), av)
            shift *= 2
        h = av * h_prev + bv
        groups.append(h)
        h_prev = h[SUBLANES - 1:SUBLANES, :]
    return jnp.concatenate(groups, axis=0), h_prev


def _token_mixer(x_ref, row0, w, pool_halo, conv_halo, h_carry):
    t_rows = x_ref.shape[0]
    row_blocks = [slice(k * t_rows // ROW_BLOCKS, (k + 1) * t_rows // ROW_BLOCKS)
                  for k in range(ROW_BLOCKS)]

    z = _rms_norm(x_ref[...], w["g_mix"][...]).astype(BF16)
    u_pool = _dot(z, w["w_in"], blocks=[0])
    u_lru = _dot(z, w["w_in"], blocks=[1])
    u_gate = _dot(z, w["w_in"], blocks=[2])
    yield

    t_idx = row0 + lax.broadcasted_iota(jnp.int32, (t_rows, 1), 0)

    ext = jnp.concatenate([pool_halo[...], u_pool], axis=0)
    pool_halo[...] = u_pool[t_rows - POOL_HALO:, :]
    d_blocks = []
    for rows in row_blocks:
        ext_rows = ext[rows.start:rows.stop + POOL_HALO]
        d_groups = []
        for g, win in enumerate(POOL_WINDOWS):
            ug = ext_rows[:, g * POOL_GROUP_WIDTH:(g + 1) * POOL_GROUP_WIDTH]
            s = ug
            k = 1
            while k < win:
                s = s + pltpu.roll(s, k, axis=0)
                k *= 2
            if rows.start < win:
                count = jnp.minimum(t_idx[rows] + 1, win).astype(F32)
                mean = s[POOL_HALO:] / count
            else:
                mean = s[POOL_HALO:] * (1.0 / win)
            d_groups.append(mean - ug[POOL_HALO:])
        d_blocks.append(jnp.concatenate(d_groups, axis=1).astype(BF16))
        yield
    d = jnp.concatenate(d_blocks, axis=0)
    y_pool = jnp.concatenate(_head_dot(d, w["w_pool"]), axis=1)
    y_pool = ((y_pool + w["pool_b"][...]) * w["pool_scale"][...]).astype(BF16)
    yield

    ext = jnp.concatenate([conv_halo[...], u_lru], axis=0)
    conv_halo[...] = u_lru[t_rows - CONV_HALO:, :]
    conv_w = w["conv_w"][...]
    xb_blocks = []
    for rows in row_blocks:
        ext_rows = ext[rows.start:rows.stop + CONV_HALO]
        xb = w["conv_b"][...]
        for k in range(CONV_WIDTH):
            lag = CONV_WIDTH - 1 - k
            tap = ext_rows if lag == 0 else pltpu.roll(ext_rows, lag, axis=0)
            xb = xb + tap * conv_w[k:k + 1, :]
        xb_blocks.append(xb[CONV_HALO:])
        yield
    xb = jnp.concatenate(xb_blocks, axis=0)

    xb_bf = xb.astype(BF16)
    log_sig_l = _log_sigmoid(w["lru_l"][...])
    y_blocks, h_last = [], h_carry[0:1, :]
    for rows in row_blocks:
        gate_parts = _head_dot(xb_bf[rows], w["w_gate"])
        r_pre = jnp.concatenate([g[:, :MXU_DIM] for g in gate_parts], axis=1)
        i_pre = jnp.concatenate([g[:, MXU_DIM:] for g in gate_parts], axis=1)
        r = jax.nn.sigmoid(r_pre + w["gate_a_b"][...])
        i = jax.nn.sigmoid(i_pre + w["gate_x_b"][...])
        a = jnp.exp(LRU_C * r * log_sig_l)
        one_minus_a2 = 1.0 - a * a
        mult = jnp.where(one_minus_a2 > 0.0, one_minus_a2 * lax.rsqrt(one_minus_a2), 0.0)
        mult = jnp.where(t_idx[rows] == 0, 1.0, mult)
        b = mult * (i * xb[rows])
        yield
        h, h_last = _linear_recurrence(a, b, h_last)
        y_blocks.append((h * jax.nn.gelu(u_gate[rows])).astype(BF16))
        yield
    h_carry[0:1, :] = h_last
    y_lru = jnp.concatenate(y_blocks, axis=0)

    h1 = x_ref[...] + (_dot(y_pool, w["w_out"], rows=slice(0, D_POOL))
                       + _dot(y_lru, w["w_out"], rows=slice(D_POOL, D_POOL + D_LRU)))
    return h1, _rms_norm(h1, w["g_mlp"][...]).astype(BF16)


def _channel_mixers(h1_ref, p_ref, w, z_buf):
    acts = []
    for c in range(D_FF // DOT_N):
        up = _dot(z_buf[...], w["w_up"], blocks=[c])
        act = jnp.maximum(up.astype(BF16), 0.0)
        acts.append(act * act)
        yield
    acc = None
    for c, act in enumerate(acts):
        down = _dot(act, w["w_down"], rows=slice(c * DOT_N, (c + 1) * DOT_N))
        acc = down if acc is None else acc + down
        yield
    h2 = h1_ref[...] + acc

    z = _rms_norm(h2, w["g_ple"][...]).astype(BF16)
    ple = _dot(p_ref[...].astype(BF16), w["w_ple_proj"])
    t_rows = z.shape[0]
    outs = []
    for k in range(PLE_ROW_BLOCKS):
        rows = slice(k * t_rows // PLE_ROW_BLOCKS, (k + 1) * t_rows // PLE_ROW_BLOCKS)
        gate = _sigmoid(_dot(z[rows], w["w_ple_gate"]) + w["b_ple_gate"][...])
        h3 = h2[rows] + gate * ple[rows]
        outs.append(_rms_norm(h3, w["g_final"][...]))
    return jnp.concatenate(outs, axis=0)


def _trace_stages(token_stages, channel_stages, order):
    generators = {"T": token_stages, "C": channel_stages}
    results = {}

    def advance(name):
        if name not in results:
            try:
                next(generators[name])
            except StopIteration as stop:
                results[name] = stop.value

    for name in order:
        advance(name)
    for name in generators:
        while name not in results:
            advance(name)
    return results["T"], results["C"]


_VMEM_PARAMS = (
    "g_mix", "pool_w", "pool_b", "pool_scale", "conv_w", "conv_b", "gate_a_w", "gate_x_w",
    "gate_a_b_heads", "gate_x_b_heads", "lru_l", "g_mlp", "g_ple", "b_ple_gate", "g_final",
)
_HBM_WEIGHTS = {
    "w_in": (D_MODEL, D_POOL + 2 * D_LRU),
    "w_out": (D_POOL + D_LRU, D_MODEL),
    "w_up": (D_MODEL, D_FF),
    "w_down": (D_FF, D_MODEL),
    "w_ple_gate": (D_MODEL, D_MODEL),
    "w_ple_proj": (PLE_DIM, D_MODEL),
}


def _load_weights(hbm, resident, stage, sems, first):
    names = list(first) + [name for name in _HBM_WEIGHTS if name not in first]
    pieces = []
    for name in names:
        k_dim, n_dim = _HBM_WEIGHTS[name]
        rows = min(k_dim, STAGE_ROWS)
        for j in range(n_dim // DOT_N):
            for r0 in range(0, k_dim, rows):
                pieces.append((hbm[name], r0, j * DOT_N, rows, resident[name], j))
        if name == first[-1]:
            pause_after = len(pieces) - 1

    def copy(i):
        src, r0, c0, rows, _, _ = pieces[i]
        slot = i % STAGE_SLOTS
        return pltpu.make_async_copy(
            src.at[0, pl.ds(r0, rows), pl.ds(c0, DOT_N)],
            stage.at[slot, pl.ds(0, rows), :], sems.at[slot])

    lookahead = STAGE_SLOTS - 1
    for i in range(min(lookahead, len(pieces))):
        copy(i).start()
    for i, (_, r0, _, rows, dst, blk) in enumerate(pieces):
        if i + lookahead < len(pieces):
            copy(i + lookahead).start()
        copy(i).wait()
        slot = i % STAGE_SLOTS

        def cast_rows(k, carry, r0=r0, dst=dst, blk=blk, slot=slot):
            off = pl.multiple_of(k * CAST_ROWS, CAST_ROWS)
            dst[blk, pl.ds(pl.multiple_of(r0 + off, CAST_ROWS), CAST_ROWS), :] = (
                stage[slot, pl.ds(off, CAST_ROWS), :].astype(BF16))
            return carry

        lax.fori_loop(0, rows // CAST_ROWS, cast_rows, 0)
        if i == pause_after:
            yield


def _place_lanes(x, offset, width):
    return jnp.pad(x, ((0, 0), (offset, width - offset - x.shape[1])))


def _assemble_head_params(w, w_pool, w_gate, gate_bias):
    def block_diag(heads_ref, first, count):
        n = heads_ref.shape[-1]
        return jnp.concatenate(
            [_place_lanes(heads_ref[first + g], g * n, count * n) for g in range(count)], axis=0)

    per_block = MXU_DIM // POOL_GROUP_WIDTH
    for blk in range(w_pool.shape[0]):
        w_pool[blk] = block_diag(w["pool_w"], blk * per_block, per_block).astype(BF16)
    per_block = MXU_DIM // LRU_BLOCK
    for blk in range(w_gate.shape[0]):
        w_gate[blk] = jnp.concatenate(
            [block_diag(w["gate_a_w"], blk * per_block, per_block),
             block_diag(w["gate_x_w"], blk * per_block, per_block)], axis=1).astype(BF16)
    for row, name in enumerate(("gate_a_b_heads", "gate_x_b_heads")):
        heads = w[name]
        pieces = [_place_lanes(heads[h:h + 1, :], h * LRU_BLOCK, D_LRU) for h in range(LRU_HEADS)]
        gate_bias[row:row + 1, :] = functools.reduce(jnp.add, pieces)


def _block_kernel(tiles_per_seq, x_ref, p_ref, *rest):
    n_vmem, n_hbm = len(_VMEM_PARAMS), len(_HBM_WEIGHTS)
    w = dict(zip(_VMEM_PARAMS, rest[:n_vmem]))
    hbm = dict(zip(_HBM_WEIGHTS, rest[n_vmem:n_vmem + n_hbm]))
    o_ref = rest[n_vmem + n_hbm]
    scratch = rest[n_vmem + n_hbm + 1:]
    resident = dict(zip(_HBM_WEIGHTS, scratch[:n_hbm]))
    (stage, sems, pool_halo, conv_halo, h_carry, h1_buf,
     w_pool, w_gate, gate_bias, z_buf) = scratch[n_hbm:]
    w.update(resident)
    w.update(w_pool=w_pool, w_gate=w_gate,
             gate_a_b=gate_bias.at[0:1, :], gate_x_b=gate_bias.at[1:2, :])
    step = pl.program_id(0)
    t_rows = x_ref.shape[0]

    tile = step % tiles_per_seq

    @pl.when(tile == 0)
    def _():
        pool_halo[...] = jnp.zeros_like(pool_halo)
        conv_halo[...] = jnp.zeros_like(conv_halo)
        h_carry[...] = jnp.zeros_like(h_carry)

    def token_mixer():
        return _token_mixer(x_ref, tile * t_rows, w, pool_halo, conv_halo, h_carry)

    @pl.when(step == 0)
    def _():
        _assemble_head_params(w, w_pool, w_gate, gate_bias)
        loader = _load_weights(hbm, resident, stage, sems, first=("w_in", "w_out"))
        next(loader)
        (h1_new, z_new), _ = _trace_stages(token_mixer(), iter(()), "")
        h1_buf[...] = h1_new
        z_buf[...] = z_new
        for _ in loader:
            pass

    last_step = pl.num_programs(0) - 1

    @pl.when(jnp.logical_and(step > 0, step < last_step))
    def _():
        (h1_new, z_new), out = _trace_stages(
            token_mixer(), _channel_mixers(h1_buf, p_ref, w, z_buf), STAGE_ORDER)
        o_ref[...] = out
        h1_buf[...] = h1_new
        z_buf[...] = z_new

    @pl.when(step == last_step)
    def _():
        _, out = _trace_stages(iter(()), _channel_mixers(h1_buf, p_ref, w, z_buf), "")
        o_ref[...] = out


def _resident(shape):
    zeros = (0,) * len(shape)
    return pl.BlockSpec(shape, lambda s: zeros, pipeline_mode=pl.Buffered(1))


@jax.jit
def kernel(x, p, norm_mix_g, w_in, pool_w, pool_b, pool_scale, conv_w, conv_b, gate_a_w, gate_a_b, gate_x_w, gate_x_b, lru_L, w_out, norm_mlp_g, w_up, w_down, norm_ple_g, w_ple_gate, b_ple_gate, w_ple_proj, norm_final_g):
    batch, seq, d_model = x.shape
    assert d_model == D_MODEL and seq % SEQ_TILE == 0
    assert (SEQ_TILE // ROW_BLOCKS) % SUBLANES == 0 and SEQ_TILE // ROW_BLOCKS >= POOL_HALO
    assert p.shape[0] == 1 and w_in.shape[0] == 1, "single-layer block"
    tiles_per_seq = seq // SEQ_TILE
    n_tiles = batch * tiles_per_seq

    row = lambda v: v.reshape(1, -1).astype(F32)
    vmem_params = dict(
        g_mix=row(norm_mix_g[0]), pool_w=pool_w[0].astype(F32), pool_b=row(pool_b[0]),
        pool_scale=row(pool_scale[0]), conv_w=conv_w[0].astype(F32), conv_b=row(conv_b[0]),
        gate_a_w=gate_a_w[0].astype(F32), gate_x_w=gate_x_w[0].astype(F32),
        gate_a_b_heads=gate_a_b[0].astype(F32), gate_x_b_heads=gate_x_b[0].astype(F32),
        lru_l=row(lru_L[0]), g_mlp=row(norm_mlp_g[0]), g_ple=row(norm_ple_g[0]),
        b_ple_gate=row(b_ple_gate[0]), g_final=row(norm_final_g),
    )
    assert vmem_params["pool_w"].shape == (len(POOL_WINDOWS), POOL_GROUP_WIDTH, POOL_GROUP_WIDTH)
    assert vmem_params["gate_a_w"].shape == (LRU_HEADS, LRU_BLOCK, LRU_BLOCK)
    vmem_args = [vmem_params[name] for name in _VMEM_PARAMS]
    hbm_weights = dict(w_in=w_in, w_out=w_out, w_up=w_up, w_down=w_down,
                       w_ple_gate=w_ple_gate, w_ple_proj=w_ple_proj)
    hbm_args = [hbm_weights[name] for name in _HBM_WEIGHTS]
    for arg, (name, shape) in zip(hbm_args, _HBM_WEIGHTS.items()):
        assert arg.shape == (1,) + shape and arg.dtype == F32, name

    def tile_spec(width, lag):
        def index_map(s):
            t = jnp.clip(s - lag, 0, n_tiles - 1)
            return (t // tiles_per_seq, t % tiles_per_seq, 0)
        return pl.BlockSpec((None, SEQ_TILE, width), index_map)

    return pl.pallas_call(
        functools.partial(_block_kernel, tiles_per_seq),
        grid=(n_tiles + 1,),
        in_specs=[tile_spec(D_MODEL, 0), tile_spec(PLE_DIM, 1)]
                 + [_resident(a.shape) for a in vmem_args]
                 + [pl.BlockSpec(memory_space=pl.ANY)] * len(hbm_args),
        out_specs=tile_spec(D_MODEL, 1),
        out_shape=jax.ShapeDtypeStruct(x.shape, x.dtype),
        scratch_shapes=[pltpu.VMEM((n // DOT_N, k, DOT_N), BF16) for k, n in _HBM_WEIGHTS.values()]
        + [
            pltpu.VMEM((STAGE_SLOTS, STAGE_ROWS, DOT_N), F32),
            pltpu.SemaphoreType.DMA((STAGE_SLOTS,)),
            pltpu.VMEM((POOL_HALO, D_POOL), F32),
            pltpu.VMEM((CONV_HALO, D_LRU), F32),
            pltpu.VMEM((SUBLANES, D_LRU), F32),
            pltpu.VMEM((SEQ_TILE, D_MODEL), F32),
            pltpu.VMEM((D_POOL // MXU_DIM, MXU_DIM, MXU_DIM), BF16),
            pltpu.VMEM((D_LRU // MXU_DIM, MXU_DIM, 2 * MXU_DIM), BF16),
            pltpu.VMEM((SUBLANES, D_LRU), F32),
            pltpu.VMEM((SEQ_TILE, D_MODEL), BF16),
        ],
        compiler_params=pltpu.CompilerParams(
            dimension_semantics=("arbitrary",),
            vmem_limit_bytes=VMEM_LIMIT_BYTES,
        ),
        name="hybrid_block",
    )(x, p[0], *vmem_args, *hbm_args)
```
